```python
import jax, jax.numpy as jnp
from jax import lax
import numpy as np

D_MODEL = 2048
BATCH = 16
SEQ = 256
DEPTH = 1
DEC_BATCH = 8
DEC_SEQ = 4096
PAST_LEN = 256

GRID_W = 64
D_A = 1024
H_B = 32
HEAD_DIM = 64
D_B = H_B * HEAD_DIM
LORA_W = 96
LORA_A = 96
LORA_G = 256
N_DIR = 2
PEER_HEADS = 8
N_KEYS = 128
N_EXPERTS = N_KEYS * N_KEYS
D_KEY = 256
HALF_KEY = D_KEY // 2
PEER_TOPK = 16
TOKEN_BLOCK = 128
RMS_EPS = 1e-6
GN_EPS = 64e-5
IN_WIDTHS = (D_A, D_A, D_A, D_B, D_B, D_B, LORA_W, LORA_W, LORA_A, LORA_A, LORA_G, D_MODEL, D_MODEL)
D_IN = 3 * D_A + 3 * D_B + 2 * LORA_W + 2 * LORA_A + LORA_G + 2 * D_MODEL

kernel_name = "hybrid_conv_rwkv7_peer_prefix_dit"


def _rmsnorm(x, g):
    xf = x.astype(jnp.float32)
    y = xf * lax.rsqrt(jnp.mean(xf * xf, axis=-1, keepdims=True) + RMS_EPS)
    return (y * g.astype(jnp.float32)).astype(x.dtype)


def _conv3(x, w, axis):
    n = x.shape[axis]
    pad = [(0, 0)] * x.ndim
    pad[axis] = (1, 1)
    xp = jnp.pad(x, pad)
    prev = lax.slice_in_dim(xp, 0, n, axis=axis)
    nxt = lax.slice_in_dim(xp, 2, n + 2, axis=axis)
    return w[0] * prev + w[1] * x + w[2] * nxt


def _short_conv(u_b, u_c, u_x, conv_w, w_out_a, rows):
    z = u_c * u_x
    if rows is None:
        zc = _conv3(z, conv_w, axis=1)
    else:
        b_, t_, ch = z.shape
        half = ch // 2
        zg = z.reshape(b_, rows, GRID_W, ch)
        zh = _conv3(zg[..., :half], conv_w[:, :half], axis=2)
        zv = _conv3(zg[..., half:], conv_w[:, half:], axis=1)
        zc = jnp.concatenate([zh, zv], axis=-1).reshape(b_, t_, ch)
    return (u_b * zc) @ w_out_a


def _wkv_scan(r, w, k, v, a_vec, b_vec, S0, reverse):
    def step(S, inp):
        r_t, w_t, k_t, v_t, a_t, b_t = inp
        sa = jnp.einsum('bhvk,bhk->bhv', S, a_t)
        S = S * w_t[:, :, None, :] + sa[..., None] * b_t[:, :, None, :] + v_t[..., None] * k_t[:, :, None, :]
        return S, jnp.einsum('bhvk,bhk->bhv', S, r_t)
    xs = tuple(jnp.moveaxis(t, 1, 0) for t in (r, w, k, v, a_vec, b_vec))
    S_fin, out = lax.scan(step, S0, xs, reverse=reverse)
    return jnp.moveaxis(out, 0, 1), S_fin


def _rwkv7(r, k, v, wd, ad, gd, S0, w0, w_up, a0, a_up, g_up, k_k, k_a, r_k, ln_g, ln_b, w_o):
    b_, t_, _ = r.shape
    heads = lambda t: t.reshape(b_, t_, H_B, HEAD_DIM)
    rh, kh, vh = heads(r), heads(k), heads(v)
    kk = (kh * k_k.reshape(H_B, HEAD_DIM)).astype(jnp.float32)
    kk = (kk * lax.rsqrt(jnp.sum(kk * kk, axis=-1, keepdims=True) + 1e-12)).astype(r.dtype)
    S0 = S0.astype(r.dtype)
    outs, bonus, finals = [], [], []
    for d in range(N_DIR):
        w_log = -jax.nn.softplus(-(w0[d] + jnp.tanh(wd[d]) @ w_up[d])) - 0.5
        decay = heads(jnp.exp(-jnp.exp(w_log)))
        a = heads(jax.nn.sigmoid(a0[d] + ad[d] @ a_up[d]))
        kd = kh * (1 + (a - 1) * k_a.reshape(H_B, HEAD_DIM))
        o, S_fin = _wkv_scan(rh, decay, kd, vh, -kk, kk * a, S0[:, d], reverse=(d == 1))
        outs.append(o)
        finals.append(S_fin)
        bonus.append(jnp.sum(rh * kd * r_k, axis=-1, keepdims=True) * vh)
    o = (outs[0] + outs[1]).astype(jnp.float32)
    mu = jnp.mean(o, axis=-1, keepdims=True)
    var = jnp.mean(jnp.square(o - mu), axis=-1, keepdims=True)
    o = ((o - mu) * lax.rsqrt(var + GN_EPS)).astype(r.dtype)
    o = o * ln_g.reshape(H_B, HEAD_DIM) + ln_b.reshape(H_B, HEAD_DIM)
    o = o + bonus[0] + bonus[1]
    g = jax.nn.sigmoid(gd) @ g_up
    y = (o.reshape(b_, t_, D_B) * g) @ w_o
    return y, jnp.stack(finals, axis=1)


def _peer(h, w_q, sub_keys, u_tab, v_tab):
    b_, t_, dm = h.shape
    hb = h.reshape(-1, TOKEN_BLOCK, dm)

    def block(hx):
        q = (hx @ w_q).reshape(TOKEN_BLOCK, PEER_HEADS, 2, HALF_KEY)
        s = jnp.einsum('thpd,phkd->thpk', q, sub_keys)
        s1, i1 = lax.top_k(s[:, :, 0], PEER_TOPK)
        s2, i2 = lax.top_k(s[:, :, 1], PEER_TOPK)
        cand = (s1[..., :, None] + s2[..., None, :]).reshape(TOKEN_BLOCK, PEER_HEADS, PEER_TOPK * PEER_TOPK)
        ids = (i1[..., :, None] * N_KEYS + i2[..., None, :]).reshape(TOKEN_BLOCK, PEER_HEADS, PEER_TOPK * PEER_TOPK)
        top_s, pos = lax.top_k(cand, PEER_TOPK)
        eid = jnp.take_along_axis(ids, pos, axis=-1)
        g = jax.nn.softmax(top_s.astype(jnp.float32), axis=-1).astype(hx.dtype)
        act = jax.nn.gelu(jnp.einsum('td,thkd->thk', hx, u_tab[eid]))
        return jnp.einsum('thk,thkd->td', g * act, v_tab[eid])

    return lax.map(block, hb).reshape(b_, t_, dm)


def _layer(x, mod, S0, rows, norm1_g, norm2_g, w_in, conv_w, w_out_a, w0, w_up, a0, a_up, g_up,
           k_k, k_a, r_k, ln_g, ln_b, w_o, w_mix, w_q, sub_keys, u_tab, v_tab):
    sh1, sc1, gt1, sh2, sc2, gt2 = jnp.split(mod, 6, axis=-1)
    h = _rmsnorm(x, norm1_g) * (1 + sc1) + sh1
    cuts = [int(i) for i in np.cumsum(IN_WIDTHS)[:-1]]
    (u_b, u_c, u_x, r, k, v, wd_f, wd_b, ad_f, ad_b, gd, gate_a, gate_b) = jnp.split(h @ w_in, cuts, axis=-1)
    y_a = _short_conv(u_b, u_c, u_x, conv_w, w_out_a, rows)
    y_b, S_fin = _rwkv7(r, k, v, (wd_f, wd_b), (ad_f, ad_b), gd, S0, w0, w_up, a0, a_up, g_up,
                        k_k, k_a, r_k, ln_g, ln_b, w_o)
    merged = jax.nn.sigmoid(gate_a) * y_a + jax.nn.sigmoid(gate_b) * y_b
    x = x + gt1 * (merged @ w_mix)
    h2 = _rmsnorm(x, norm2_g) * (1 + sc2) + sh2
    x = x + gt2 * _peer(h2, w_q, sub_keys, u_tab, v_tab)
    return x, S_fin


def setup_inputs(seed: int = 0) -> dict:
    key = jax.random.key(seed)
    ks = iter(jax.random.split(key, 32))
    nrm = lambda shape, s: jax.random.normal(next(ks), shape, jnp.float32) * s
    L = DEPTH
    return {
        "x_prompt": nrm((BATCH, SEQ, D_MODEL), 1.0),
        "x_sample": nrm((DEC_BATCH, DEC_SEQ, D_MODEL), 1.0),
        "state_rwkv": nrm((DEC_BATCH, L, N_DIR, H_B, HEAD_DIM, HEAD_DIM), 0.5),
        "c": nrm((DEC_BATCH, D_MODEL), 1.0),
        "c_ctx": nrm((D_MODEL,), 1.0),
        "w_ada": nrm((L, D_MODEL, 6 * D_MODEL), 0.5 * D_MODEL ** -0.5),
        "b_ada": nrm((L, 6 * D_MODEL), 0.01),
        "norm1_g": 1.0 + nrm((L, D_MODEL), 0.1),
        "norm2_g": 1.0 + nrm((L, D_MODEL), 0.1),
        "w_in": nrm((L, D_MODEL, D_IN), D_MODEL ** -0.5),
        "conv_w": nrm((L, 3, D_A), 0.5),
        "w_out_a": nrm((L, D_A, D_MODEL), D_A ** -0.5),
        "w0": nrm((L, N_DIR, D_B), 1.0),
        "w_up": nrm((L, N_DIR, LORA_W, D_B), 0.5 * LORA_W ** -0.5),
        "a0": nrm((L, N_DIR, D_B), 0.1),
        "a_up": nrm((L, N_DIR, LORA_A, D_B), 0.5 * LORA_A ** -0.5),
        "g_up": nrm((L, LORA_G, D_B), LORA_G ** -0.5),
        "k_k": 0.85 + nrm((L, D_B), 0.05),
        "k_a": 1.0 + nrm((L, D_B), 0.05),
        "r_k": nrm((L, H_B, HEAD_DIM), 0.1),
        "ln_g": 1.0 + nrm((L, D_B), 0.1),
        "ln_b": nrm((L, D_B), 0.01),
        "w_o": nrm((L, D_B, D_MODEL), D_B ** -0.5),
        "w_mix": nrm((L, D_MODEL, D_MODEL), D_MODEL ** -0.5),
        "w_q": nrm((L, D_MODEL, PEER_HEADS * D_KEY), D_MODEL ** -0.5),
        "sub_keys": nrm((L, 2, PEER_HEADS, N_KEYS, HALF_KEY), HALF_KEY ** -0.5),
        "u_tab": nrm((L, N_EXPERTS, D_MODEL), D_MODEL ** -0.5),
        "v_tab": nrm((L, N_EXPERTS, D_MODEL), 0.5),
        "final_g": 1.0 + nrm((D_MODEL,), 0.1),
    }


def reference(x_prompt, x_sample, state_rwkv, c, c_ctx, w_ada, b_ada, norm1_g, norm2_g, w_in, conv_w,
              w_out_a, w0, w_up, a0, a_up, g_up, k_k, k_a, r_k, ln_g, ln_b, w_o, w_mix, w_q, sub_keys,
              u_tab, v_tab, final_g):
    rows = x_sample.shape[1] // GRID_W
    xp, xs = x_prompt, x_sample
    new_states = []
    for l in range(DEPTH):
        lp = (norm1_g[l], norm2_g[l], w_in[l], conv_w[l], w_out_a[l], w0[l], w_up[l], a0[l], a_up[l],
              g_up[l], k_k[l], k_a[l], r_k[l], ln_g[l], ln_b[l], w_o[l], w_mix[l], w_q[l], sub_keys[l],
              u_tab[l], v_tab[l])
        mod_ctx = (jax.nn.silu(c_ctx) @ w_ada[l] + b_ada[l])[None, None, :]
        mod_lat = (jax.nn.silu(c) @ w_ada[l] + b_ada[l])[:, None, :]
        S0_ctx = jnp.zeros((xp.shape[0], N_DIR, H_B, HEAD_DIM, HEAD_DIM), xp.dtype)
        xp, S_ctx = _layer(xp, mod_ctx, S0_ctx, None, *lp)
        new_states.append(S_ctx)
        xs, _ = _layer(xs, mod_lat, state_rwkv[:, l], rows, *lp)
    y_prompt = _rmsnorm(xp, final_g)
    y_sample = _rmsnorm(xs, final_g)
    new_state_rwkv = jnp.stack(new_states, axis=1)
    return (y_prompt, y_sample, new_state_rwkv)
```

```python
import functools
import math

import jax
import jax.numpy as jnp
from jax import lax
from jax.experimental import pallas as pl
from jax.experimental.pallas import tpu as pltpu

F32 = jnp.float32
BF16 = jnp.bfloat16

D_MODEL = 2048
GRID_W = 64
D_A = 1024
H_B = 32
HEAD_DIM = 64
D_B = H_B * HEAD_DIM
LORA_W = 96
LORA_A = 96
LORA_G = 256
PEER_HEADS = 8
N_KEYS = 128
HALF_KEY = 128
PEER_TOPK = 16
RMS_EPS = 1e-6
GN_EPS = 64e-5

LANE = 128
CHUNK = 64
SCAN_BLOCK = 256
GROUP = 256
HEADS_PER_GROUP = GROUP // HEAD_DIM
N_GROUPS = D_B // GROUP
LORA_PAD = 128
EXPERTS_PER_TOKEN = PEER_HEADS * PEER_TOPK
GATHER_TOKENS = 8
VMEM_LIMIT = 56 * 1024 * 1024

COL_R, COL_K, COL_V, COL_GA, COL_GB = 0, 2048, 4096, 6144, 8192
COL_UB, COL_UC, COL_UX, COL_LORA = 10240, 11264, 12288, 13312
D_IN_PAD = 14336
LORA_BLOCK = 1024


def _cparams(sem, vmem=VMEM_LIMIT):
    return pltpu.CompilerParams(dimension_semantics=sem, vmem_limit_bytes=vmem)


def _dot(a, b):
    return jnp.dot(a, b, preferred_element_type=F32)


def _dot_nt(a, b):
    return lax.dot_general(a, b, (((1,), (1,)), ((), ())), preferred_element_type=F32)


def _dot_tn(a, b):
    return lax.dot_general(a, b, (((0,), (0,)), ((), ())), preferred_element_type=F32)


def _split(x, parts):
    out = []
    for _ in range(parts):
        p = x.astype(BF16)
        out.append(p)
        x = x - p.astype(F32)
    return out


def _head_ones():
    r = lax.broadcasted_iota(jnp.int32, (GROUP, GROUP), 0) // HEAD_DIM
    c = lax.broadcasted_iota(jnp.int32, (GROUP, GROUP), 1) // HEAD_DIM
    return r == c


def _headsum(x, ones_bf):
    outs = []
    for g in range(x.shape[1] // GROUP):
        xs = x[:, g * GROUP:(g + 1) * GROUP]
        hi, lo = _split(xs, 2)
        outs.append(_dot(hi, ones_bf) + _dot(lo, ones_bf))
    return outs[0] if len(outs) == 1 else jnp.concatenate(outs, axis=1)


def _rms(x):
    return x * lax.rsqrt(jnp.mean(x * x, axis=-1, keepdims=True) + RMS_EPS)


def _mod_body(c_ref, w_ref, b_ref, o_ref):
    c = c_ref[...]
    s = (c * jax.nn.sigmoid(c)).astype(BF16)
    o_ref[...] = _dot(s, w_ref[...]) + b_ref[...]


def _stage_mod(cc, w_ada_bf, b_ada):
    rows, d = cc.shape
    n = w_ada_bf.shape[1]
    tn = 2048
    return pl.pallas_call(
        _mod_body,
        out_shape=jax.ShapeDtypeStruct((rows, n), F32),
        grid=(n // tn,),
        in_specs=[pl.BlockSpec((rows, d), lambda j: (0, 0)),
                  pl.BlockSpec((d, tn), lambda j: (0, j)),
                  pl.BlockSpec((1, tn), lambda j: (0, j))],
        out_specs=pl.BlockSpec((rows, tn), lambda j: (0, j)),
        compiler_params=_cparams(("arbitrary",)),
        name="adaln_mod",
    )(cc, w_ada_bf, b_ada)


def _mod_spec(which, tm, n_ctx_tok, lat_len):
    nct = n_ctx_tok // tm
    per = lat_len // tm

    def imap(i, *_):
        row = jnp.where(i < nct, 0, 1 + (i - nct) // per)
        return (row, which, 0, 0)

    return pl.BlockSpec((None, None, 1, D_MODEL), imap)


def _inproj_body(x_ref, sh_ref, sc_ref, g_ref, w_ref, o_ref, h_ref):
    @pl.when(pl.program_id(1) == 0)
    def _():
        h = _rms(x_ref[...]) * g_ref[...] * (1.0 + sc_ref[...]) + sh_ref[...]
        h_ref[...] = h.astype(BF16)

    o_ref[...] = _dot(h_ref[...], w_ref[...])


def _stage_inproj(x_all, mod4, norm1_g, w_cat, n_ctx_tok, lat_len):
    t = x_all.shape[0]
    tm, tn = 512, 2048
    return pl.pallas_call(
        _inproj_body,
        out_shape=jax.ShapeDtypeStruct((t, D_IN_PAD), F32),
        grid=(t // tm, D_IN_PAD // tn),
        in_specs=[pl.BlockSpec((tm, D_MODEL), lambda i, j: (i, 0)),
                  _mod_spec(0, tm, n_ctx_tok, lat_len),
                  _mod_spec(1, tm, n_ctx_tok, lat_len),
                  pl.BlockSpec((1, D_MODEL), lambda i, j: (0, 0)),
                  pl.BlockSpec((D_MODEL, tn), lambda i, j: (0, j))],
        out_specs=pl.BlockSpec((tm, tn), lambda i, j: (i, j)),
        scratch_shapes=[pltpu.VMEM((tm, D_MODEL), BF16)],
        compiler_params=_cparams(("parallel", "arbitrary")),
        name="norm1_inproj",
    )(x_all, mod4, mod4, norm1_g, w_cat)


CONV_TILE = 256


def _conv_body(n_ctx_tiles, tiles_per_lat, ub_ref, uc_ref, ux_ref, pc_ref, px_ref, nc_ref, nx_ref,
               cw_ref, wo_ref, y_ref):
    i = pl.program_id(0)
    is_lat = i >= n_ctx_tiles
    jl = (i - n_ctx_tiles) % tiles_per_lat
    half = D_A // 2
    z = uc_ref[...] * ux_ref[...]
    cw = cw_ref[...]
    row = lax.broadcasted_iota(jnp.int32, (CONV_TILE, 1), 0)
    pmask = jnp.where(is_lat, GRID_W - 1, CONV_TILE - 1)
    keep_p = (row & pmask) != 0
    keep_n = (row & pmask) != pmask
    zp = jnp.where(keep_p, pltpu.roll(z, 1, 0), 0.0)
    zn = jnp.where(keep_n, pltpu.roll(z, CONV_TILE - 1, 0), 0.0)
    conv_h = cw[0:1] * zp + cw[1:2] * z + cw[2:3] * zn
    zv = z[:, half:]
    hp = jnp.where(jl != 0, pc_ref[...] * px_ref[...], 0.0)
    hn = jnp.where(jl != tiles_per_lat - 1, nc_ref[...] * nx_ref[...], 0.0)
    vp = jnp.concatenate([hp, zv[:CONV_TILE - GRID_W]], axis=0)
    vn = jnp.concatenate([zv[GRID_W:], hn], axis=0)
    conv_v = cw[0:1, half:] * vp + cw[1:2, half:] * zv + cw[2:3, half:] * vn
    zc = jnp.concatenate([conv_h[:, :half], jnp.where(is_lat, conv_v, conv_h[:, half:])], axis=1)
    y_ref[...] = _dot((ub_ref[...] * zc).astype(BF16), wo_ref[...])


def _stage_conv(u, conv_w, w_out_a_bf, n_ctx_tok, lat_len):
    t = u.shape[0]
    tm = CONV_TILE
    n_ctx_tiles = n_ctx_tok // tm
    tiles_per_lat = lat_len // tm
    rpt = tm // GRID_W
    last_halo = t // GRID_W - 1
    half = D_A // 2
    cb = lambda col: col // D_A
    hb = lambda col: (col + half) // half
    body = functools.partial(_conv_body, n_ctx_tiles, tiles_per_lat)
    prev_map = lambda col: (lambda i: (jnp.maximum(i * rpt - 1, 0), hb(col)))
    next_map = lambda col: (lambda i: (jnp.minimum(i * rpt + rpt, last_halo), hb(col)))
    return pl.pallas_call(
        body,
        out_shape=jax.ShapeDtypeStruct((t, D_MODEL), F32),
        grid=(t // tm,),
        in_specs=[pl.BlockSpec((tm, D_A), lambda i: (i, cb(COL_UB))),
                  pl.BlockSpec((tm, D_A), lambda i: (i, cb(COL_UC))),
                  pl.BlockSpec((tm, D_A), lambda i: (i, cb(COL_UX))),
                  pl.BlockSpec((GRID_W, half), prev_map(COL_UC)),
                  pl.BlockSpec((GRID_W, half), prev_map(COL_UX)),
                  pl.BlockSpec((GRID_W, half), next_map(COL_UC)),
                  pl.BlockSpec((GRID_W, half), next_map(COL_UX)),
                  pl.BlockSpec((3, D_A), lambda i: (0, 0)),
                  pl.BlockSpec((D_A, D_MODEL), lambda i: (0, 0))],
        out_specs=pl.BlockSpec((tm, D_MODEL), lambda i: (i, 0)),
        compiler_params=_cparams(("parallel",)),
        name="short_conv",
    )(u, u, u, u, u, u, u, conv_w, w_out_a_bf)


def _rwkv_pre_body(lora_ref, w0_ref, wup_ref, a0_ref, aup_ref, lwf_ref, lwb_ref, alf_ref, alb_ref):
    lo = lora_ref[...]
    lw_refs = (lwf_ref, lwb_ref)
    al_refs = (alf_ref, alb_ref)
    for d in range(2):
        wd = lo[:, d * LORA_PAD:(d + 1) * LORA_PAD]
        ad = lo[:, (2 + d) * LORA_PAD:(3 + d) * LORA_PAD]
        xw = w0_ref[d:d + 1, :] + _dot(jnp.tanh(wd).astype(BF16), wup_ref[d])
        lw_refs[d][...] = -math.exp(-0.5) * jax.nn.sigmoid(xw)
        al_refs[d][...] = jax.nn.sigmoid(a0_ref[d:d + 1, :] + _dot(ad.astype(BF16), aup_ref[d]))


def _stage_rwkv_pre(u, w0, w_up_bf, a0, a_up_bf):
    t = u.shape[0]
    tm = 512
    out = jax.ShapeDtypeStruct((t, D_B), F32)
    ospec = pl.BlockSpec((tm, D_B), lambda i: (i, 0))
    return pl.pallas_call(
        _rwkv_pre_body,
        out_shape=(out, out, out, out),
        grid=(t // tm,),
        in_specs=[pl.BlockSpec((tm, LORA_BLOCK), lambda i: (i, COL_LORA // LORA_BLOCK)),
                  pl.BlockSpec((2, D_B), lambda i: (0, 0)),
                  pl.BlockSpec((2, LORA_PAD, D_B), lambda i: (0, 0, 0)),
                  pl.BlockSpec((2, D_B), lambda i: (0, 0)),
                  pl.BlockSpec((2, LORA_PAD, D_B), lambda i: (0, 0, 0))],
        out_specs=(ospec, ospec, ospec, ospec),
        compiler_params=_cparams(("parallel",)),
        name="rwkv_pre",
    )(u, w0, w_up_bf, a0, a_up_bf)


def _scan_unit(reverse, r, k, v, lw, al, h_state, kk_p, ka_p, rk_p, c):
    ones_bf, bdmask_bf, bdmask, tri_bf, strict4, incl4, eye4, ones_tn = c

    def bd(x_bf):
        return jnp.concatenate([x_bf] * HEADS_PER_GROUP, axis=0) * bdmask_bf

    kraw = k * kk_p
    kkn = kraw * lax.rsqrt(_headsum(kraw * kraw, ones_bf) + 1e-12)
    kd = k * (1.0 + (al - 1.0) * ka_p)
    a = -kkn
    b = kkn * al

    lw_parts = _split(lw, 3)
    lc = sum(_dot(tri_bf, p) for p in lw_parts)
    ltot_col = sum(_dot_tn(p, ones_tn) for p in lw_parts)
    ltot_row = lc[0:1] if reverse else lc[CHUNK - 1:CHUNK]
    rt = r * jnp.exp(lc)
    at = a * jnp.exp(lc - lw)
    einv = jnp.exp(-lc)
    bh = b * einv
    kh = kd * einv
    etail = jnp.exp(ltot_row - lc)
    bt = b * etail
    kt = kd * etail

    lhs1 = jnp.concatenate([at, rt], axis=0).astype(BF16)
    ab = _dot_nt(lhs1, bd(bh.astype(BF16)))
    ak = _dot_nt(lhs1, bd(kh.astype(BF16)))
    a_ab = jnp.where(strict4, ab[:CHUNK], 0.0)
    a_rb = jnp.where(incl4, ab[CHUNK:], 0.0)
    a_ak = jnp.where(strict4, ak[:CHUNK], 0.0)
    a_rk = jnp.where(incl4, ak[CHUNK:], 0.0)

    x = a_ab
    tinv = eye4 + x
    x = _dot(x.astype(BF16), bd(x.astype(BF16)))
    for _ in range(4):
        y = _dot(jnp.concatenate([x, tinv], axis=0).astype(BF16), bd(x.astype(BF16)))
        x = y[:CHUNK]
        tinv = tinv + y[CHUNK:]
    tinv = tinv + _dot(tinv.astype(BF16), bd(x.astype(BF16)))

    v_bf = v.astype(BF16)
    bdv = bd(v_bf)
    p = _dot(lhs1, h_state.astype(BF16))
    rhs = p[:CHUNK] + _dot(a_ak.astype(BF16), bdv)
    u = _dot(tinv.astype(BF16), bd(rhs.astype(BF16)))
    u_bf = u.astype(BF16)
    o = p[CHUNK:] + _dot(jnp.concatenate([a_rb, a_rk], axis=1).astype(BF16),
                         jnp.concatenate([bd(u_bf), bdv], axis=0))
    upd = _dot_tn(jnp.concatenate([bt, kt], axis=0).astype(BF16),
                  jnp.concatenate([u_bf, v_bf], axis=0))
    h_new = jnp.exp(ltot_col) * h_state + jnp.where(bdmask, upd, 0.0)
    bonus = _headsum(r * kd * rk_p, ones_bf) * v
    return o, bonus, h_new


def _scan_body(reverse, has_prev, n_ctx_blk, blk_per_ctx, blk_per_lat, n_blk, *refs):
    if has_prev:
        (r_ref, k_ref, v_ref, lw_ref, al_ref, s0_ref, kk_ref, ka_ref, rk_ref, op_ref, bp_ref,
         o_ref, bon_ref, sfin_ref, h_ref) = refs
    else:
        (r_ref, k_ref, v_ref, lw_ref, al_ref, s0_ref, kk_ref, ka_ref, rk_ref,
         o_ref, bon_ref, sfin_ref, h_ref) = refs
        op_ref = bp_ref = None
    s = pl.program_id(1)
    j = (n_blk - 1 - s) if reverse else s
    is_ctx = j < n_ctx_blk
    first_c = (blk_per_ctx - 1) if reverse else 0
    first_l = (blk_per_lat - 1) if reverse else 0
    start = jnp.where(is_ctx, (j % blk_per_ctx) == first_c, ((j - n_ctx_blk) % blk_per_lat) == first_l)

    @pl.when(start)
    def _():
        h_ref[...] = jnp.where(is_ctx, 0.0, s0_ref[...])

    ones_mask = _head_ones()
    ones_bf = ones_mask.astype(F32).astype(BF16)
    t_i = lax.broadcasted_iota(jnp.int32, (CHUNK, CHUNK), 0)
    s_i = lax.broadcasted_iota(jnp.int32, (CHUNK, CHUNK), 1)
    tri = (s_i >= t_i) if reverse else (s_i <= t_i)
    tri_bf = tri.astype(F32).astype(BF16)
    t4 = lax.broadcasted_iota(jnp.int32, (CHUNK, GROUP), 0)
    s4 = lax.broadcasted_iota(jnp.int32, (CHUNK, GROUP), 1) % HEAD_DIM
    incl4 = (s4 >= t4) if reverse else (s4 <= t4)
    strict4 = (s4 > t4) if reverse else (s4 < t4)
    eye4 = (s4 == t4).astype(F32)
    ones_tn = jnp.ones((CHUNK, GROUP), BF16)
    consts = (ones_bf, ones_bf, ones_mask, tri_bf, strict4, incl4, eye4, ones_tn)

    kk_p = kk_ref[...]
    ka_p = ka_ref[...]
    rk_p = rk_ref[...]
    h_state = h_ref[...]
    n_chunks = SCAN_BLOCK // CHUNK
    order = range(n_chunks - 1, -1, -1) if reverse else range(n_chunks)
    for cidx in order:
        sl = slice(cidx * CHUNK, (cidx + 1) * CHUNK)
        o, bonus, h_state = _scan_unit(reverse, r_ref[sl, :], k_ref[sl, :], v_ref[sl, :],
                                       lw_ref[sl, :], al_ref[sl, :], h_state, kk_p, ka_p, rk_p, consts)
        if has_prev:
            o = o + op_ref[sl, :]
            bonus = bonus + bp_ref[sl, :]
        o_ref[sl, :] = o
        bon_ref[sl, :] = bonus
    h_ref[...] = h_state
    sfin_ref[...] = h_state


def _stage_scan(reverse, u, lw, al, s0_bd, k_k, k_a, r_k, prev, n_ctx_seq, ctx_len, n_lat_seq, lat_len):
    t = u.shape[0]
    blk_per_ctx = ctx_len // SCAN_BLOCK
    blk_per_lat = lat_len // SCAN_BLOCK
    n_ctx_blk = n_ctx_seq * blk_per_ctx
    n_blk = t // SCAN_BLOCK
    has_prev = prev is not None

    def jmap(s):
        return (n_blk - 1 - s) if reverse else s

    def tok(colbase):
        return pl.BlockSpec((SCAN_BLOCK, GROUP), lambda g, s: (jmap(s), colbase // GROUP + g))

    def s0_map(g, s):
        j = jmap(s)
        b = jnp.clip((j - n_ctx_blk) // blk_per_lat, 0, n_lat_seq - 1)
        return (b, g, 0, 0)

    def sfin_map(g, s):
        j = jmap(s)
        b = jnp.where(j < n_ctx_blk, j // blk_per_ctx, n_ctx_seq)
        return (b, g, 0, 0)

    par = pl.BlockSpec((1, GROUP), lambda g, s: (0, g))
    in_specs = [tok(COL_R), tok(COL_K), tok(COL_V), tok(0), tok(0),
                pl.BlockSpec((None, None, GROUP, GROUP), s0_map), par, par, par]
    args = [u, u, u, lw, al, s0_bd, k_k, k_a, r_k]
    if has_prev:
        in_specs += [tok(0), tok(0)]
        args += list(prev)
    out = jax.ShapeDtypeStruct((t, D_B), F32)
    body = functools.partial(_scan_body, reverse, has_prev, n_ctx_blk, blk_per_ctx, blk_per_lat, n_blk)
    return pl.pallas_call(
        body,
        out_shape=(out, out, jax.ShapeDtypeStruct((n_ctx_seq + 1, N_GROUPS, GROUP, GROUP), F32)),
        grid=(N_GROUPS, n_blk),
        in_specs=in_specs,
        out_specs=(tok(0), tok(0), pl.BlockSpec((None, None, GROUP, GROUP), sfin_map)),
        scratch_shapes=[pltpu.VMEM((GROUP, GROUP), F32)],
        compiler_params=_cparams(("parallel", "arbitrary")),
        name="rwkv_scan_bwd" if reverse else "rwkv_scan_fwd",
    )(*args)


def _post_body(o_ref, bon_ref, lora_ref, lng_ref, lnb_ref, gup_ref, wo_ref, y_ref):
    ones_bf = _head_ones().astype(F32).astype(BF16)
    o = o_ref[...]
    mu = _headsum(o, ones_bf) * (1.0 / HEAD_DIM)
    d = o - mu
    var = _headsum(d * d, ones_bf) * (1.0 / HEAD_DIM)
    on = d * lax.rsqrt(var + GN_EPS) * lng_ref[...] + lnb_ref[...] + bon_ref[...]
    gd = lora_ref[:, 4 * LORA_PAD:4 * LORA_PAD + LORA_G]
    g = _dot(jax.nn.sigmoid(gd).astype(BF16), gup_ref[...])
    y_ref[...] = _dot((on * g).astype(BF16), wo_ref[...])


def _stage_post(o, bon, u, ln_g, ln_b, g_up_bf, w_o_bf):
    t = o.shape[0]
    tm = 256
    return pl.pallas_call(
        _post_body,
        out_shape=jax.ShapeDtypeStruct((t, D_MODEL), F32),
        grid=(t // tm,),
        in_specs=[pl.BlockSpec((tm, D_B), lambda i: (i, 0)),
                  pl.BlockSpec((tm, D_B), lambda i: (i, 0)),
                  pl.BlockSpec((tm, LORA_BLOCK), lambda i: (i, COL_LORA // LORA_BLOCK)),
                  pl.BlockSpec((1, D_B), lambda i: (0, 0)),
                  pl.BlockSpec((1, D_B), lambda i: (0, 0)),
                  pl.BlockSpec((LORA_G, D_B), lambda i: (0, 0)),
                  pl.BlockSpec((D_B, D_MODEL), lambda i: (0, 0))],
        out_specs=pl.BlockSpec((tm, D_MODEL), lambda i: (i, 0)),
        compiler_params=_cparams(("parallel",)),
        name="rwkv_post",
    )(o, bon, u, ln_g, ln_b, g_up_bf, w_o_bf)


def _merge_body(ya_ref, yb_ref, ga_ref, gb_ref, x_ref, gt_ref, sh_ref, sc_ref, g_ref, w_ref, x1_ref, h2_ref):
    m = jax.nn.sigmoid(ga_ref[...]) * ya_ref[...] + jax.nn.sigmoid(gb_ref[...]) * yb_ref[...]
    x1 = x_ref[...] + gt_ref[...] * _dot(m.astype(BF16), w_ref[...])
    x1_ref[...] = x1
    h2_ref[...] = _rms(x1) * g_ref[...] * (1.0 + sc_ref[...]) + sh_ref[...]


def _stage_merge(y_a, y_b, u, x_all, mod4, norm2_g, w_mix_bf, n_ctx_tok, lat_len):
    t = x_all.shape[0]
    tm = 256
    full = lambda col: pl.BlockSpec((tm, D_MODEL), lambda i: (i, col // D_MODEL))
    out = jax.ShapeDtypeStruct((t, D_MODEL), F32)
    return pl.pallas_call(
        _merge_body,
        out_shape=(out, out),
        grid=(t // tm,),
        in_specs=[full(0), full(0), full(COL_GA), full(COL_GB), full(0),
                  _mod_spec(2, tm, n_ctx_tok, lat_len),
                  _mod_spec(3, tm, n_ctx_tok, lat_len),
                  _mod_spec(4, tm, n_ctx_tok, lat_len),
                  pl.BlockSpec((1, D_MODEL), lambda i: (0, 0)),
                  pl.BlockSpec((D_MODEL, D_MODEL), lambda i: (0, 0))],
        out_specs=(full(0), full(0)),
        compiler_params=_cparams(("parallel",)),
        name="merge_norm2",
    )(y_a, y_b, u, u, x_all, mod4, mod4, mod4, norm2_g, w_mix_bf)


TOPK_TOKENS = 256


def _extract_topk(s, pos, limit, payload=None):
    vals, outs = [], []
    for _ in range(PEER_TOPK):
        m = jnp.max(s, axis=0, keepdims=True)
        pm = jnp.min(jnp.where(s == m, pos, limit), axis=0, keepdims=True)
        sel = pos == pm
        vals.append(m)
        if payload is None:
            outs.append(pm)
        else:
            outs.append(jnp.max(jnp.where(sel, payload, -1), axis=0, keepdims=True))
        s = jnp.where(sel, -jnp.inf, s)
    return jnp.concatenate(vals, axis=0), jnp.concatenate(outs, axis=0)


def _topk_body(h_ref, wq_ref, keys_ref, eid_ref, gate_ref, q_ref):
    q_ref[...] = _dot_nt(wq_ref[...], h_ref[...].astype(BF16))
    kpos = lax.broadcasted_iota(jnp.int32, (N_KEYS, TOPK_TOKENS), 0)
    cpos = lax.broadcasted_iota(jnp.int32, (PEER_TOPK * PEER_TOPK, TOPK_TOKENS), 0)

    def head(hd, carry):
        tops = []
        for p in range(2):
            off = pl.multiple_of(hd * (2 * HALF_KEY) + p * HALF_KEY, HALF_KEY)
            qs = q_ref[pl.ds(off, HALF_KEY), :].astype(BF16)
            s = _dot(keys_ref[p * PEER_HEADS + hd], qs)
            tops.append(_extract_topk(s, kpos, N_KEYS))
        (s1, i1), (s2, i2) = tops
        cand = jnp.concatenate([s1[i:i + 1] + s2 for i in range(PEER_TOPK)], axis=0)
        ids = jnp.concatenate([i1[i:i + 1] * N_KEYS + i2 for i in range(PEER_TOPK)], axis=0)
        tv, te = _extract_topk(cand, cpos, PEER_TOPK * PEER_TOPK, payload=ids)
        ex = jnp.exp(tv - tv[0:1])
        gate = ex / jnp.sum(ex, axis=0, keepdims=True)
        row = pl.multiple_of(hd * PEER_TOPK, PEER_TOPK)
        eid_ref[pl.ds(row, PEER_TOPK), :] = te
        gate_ref[pl.ds(row, PEER_TOPK), :] = gate
        return carry

    lax.fori_loop(0, PEER_HEADS, head, 0)


def _stage_topk(h2, w_q_t_bf, keys_bf):
    t = h2.shape[0]
    tb = TOPK_TOKENS
    return pl.pallas_call(
        _topk_body,
        out_shape=(jax.ShapeDtypeStruct((EXPERTS_PER_TOKEN, t), jnp.int32),
                   jax.ShapeDtypeStruct((EXPERTS_PER_TOKEN, t), F32)),
        grid=(t // tb,),
        in_specs=[pl.BlockSpec((tb, D_MODEL), lambda i: (i, 0)),
                  pl.BlockSpec((PEER_HEADS * 2 * HALF_KEY, D_MODEL), lambda i: (0, 0)),
                  pl.BlockSpec((2 * PEER_HEADS, N_KEYS, HALF_KEY), lambda i: (0, 0, 0))],
        out_specs=(pl.BlockSpec((EXPERTS_PER_TOKEN, tb), lambda i: (0, i)),
                   pl.BlockSpec((EXPERTS_PER_TOKEN, tb), lambda i: (0, i))),
        scratch_shapes=[pltpu.VMEM((PEER_HEADS * 2 * HALF_KEY, tb), F32)],
        compiler_params=_cparams(("parallel",)),
        name="peer_topk",
    )(h2, w_q_t_bf, keys_bf)


def _gather_copy(tab_ref, buf, sem, slot, row, expert):
    return pltpu.make_async_copy(tab_ref.at[expert], buf.at[slot, pl.ds(row, 1), :], sem.at[slot])


def _gather_body(ids_ref, idn_ref, gate_ref, h_ref, x_ref, gt_ref, fg_ref, tab_ref, o_ref, buf, sem):
    i = pl.program_id(0)
    n = pl.num_programs(0)
    slot = i % 2
    tg = GATHER_TOKENS
    ne = EXPERTS_PER_TOKEN

    def issue(idref, sl):
        def tok(t, carry):
            for e in range(ne):
                _gather_copy(tab_ref, buf, sem, sl, t * ne + e, idref[t, e]).start()
            return carry
        lax.fori_loop(0, tg, tok, 0)

    @pl.when(i == 0)
    def _():
        issue(ids_ref, 0)

    @pl.when(i + 1 < n)
    def _():
        issue(idn_ref, 1 - slot)

    def wtok(t, carry):
        for e in range(ne):
            _gather_copy(tab_ref, buf, sem, slot, t * ne + e, 0).wait()
        return carry
    lax.fori_loop(0, tg, wtok, 0)

    xb = h_ref[...].astype(BF16)
    gate = gate_ref[...]
    rowio = lax.broadcasted_iota(jnp.int32, (tg, ne), 0)
    acc = jnp.zeros((tg, D_MODEL), F32)
    for t in range(tg):
        u_rows = buf[slot, t * ne:(t + 1) * ne, :D_MODEL]
        v_rows = buf[slot, t * ne:(t + 1) * ne, D_MODEL:]
        pre = _dot_nt(xb, u_rows.astype(BF16))
        coef = jnp.where(rowio == t, gate * jax.nn.gelu(pre, approximate=True), 0.0)
        acc = acc + _dot(coef.astype(BF16), v_rows.astype(BF16))
    x2 = x_ref[...] + gt_ref[...] * acc
    o_ref[...] = _rms(x2) * fg_ref[...]


def _stage_gather(eid, gate, h2, x1, mod4, final_g, tab, n_ctx_tok, lat_len):
    t = h2.shape[0]
    tg = GATHER_TOKENS
    n = t // tg
    tokspec = lambda w: pl.BlockSpec((tg, w), lambda i: (i, 0))
    return pl.pallas_call(
        _gather_body,
        out_shape=jax.ShapeDtypeStruct((t, D_MODEL), F32),
        grid=(n,),
        in_specs=[pl.BlockSpec((tg, EXPERTS_PER_TOKEN), lambda i: (i, 0), memory_space=pltpu.SMEM),
                  pl.BlockSpec((tg, EXPERTS_PER_TOKEN), lambda i: (jnp.minimum(i + 1, n - 1), 0),
                               memory_space=pltpu.SMEM),
                  tokspec(EXPERTS_PER_TOKEN), tokspec(D_MODEL), tokspec(D_MODEL),
                  _mod_spec(5, tg, n_ctx_tok, lat_len),
                  pl.BlockSpec((1, D_MODEL), lambda i: (0, 0)),
                  pl.BlockSpec(memory_space=pl.ANY)],
        out_specs=tokspec(D_MODEL),
        scratch_shapes=[pltpu.VMEM((2, tg * EXPERTS_PER_TOKEN, 2 * D_MODEL), F32),
                        pltpu.SemaphoreType.DMA((2,))],
        compiler_params=_cparams(("arbitrary",)),
        name="peer_gather",
    )(eid, eid, gate, h2, x1, mod4, final_g, tab)


def _pad_rows(w, rows):
    return jnp.pad(w, ((0, 0), (0, rows - w.shape[1]), (0, 0)))


def _layer_weights(w_in, w_up, a_up):
    widths = (D_A, D_A, D_A, D_B, D_B, D_B, LORA_W, LORA_W, LORA_A, LORA_A, LORA_G, D_MODEL, D_MODEL)
    cuts = [0]
    for w in widths:
        cuts.append(cuts[-1] + w)
    seg = [w_in[:, cuts[i]:cuts[i + 1]] for i in range(len(widths))]
    (u_b, u_c, u_x, r, k, v, wd_f, wd_b, ad_f, ad_b, gd, gate_a, gate_b) = seg
    padc = lambda m: jnp.pad(m, ((0, 0), (0, LORA_PAD - m.shape[1])))
    tail = jnp.zeros((w_in.shape[0], LORA_BLOCK - 4 * LORA_PAD - LORA_G), w_in.dtype)
    w_cat = jnp.concatenate([r, k, v, gate_a, gate_b, u_b, u_c, u_x,
                             padc(wd_f), padc(wd_b), padc(ad_f), padc(ad_b), gd, tail], axis=1)
    return w_cat.astype(BF16), _pad_rows(w_up, LORA_PAD).astype(BF16), _pad_rows(a_up, LORA_PAD).astype(BF16)


def _state_to_blockdiag(s0):
    b = s0.shape[0]
    st = jnp.swapaxes(s0, -1, -2).reshape(b, N_GROUPS, HEADS_PER_GROUP, HEAD_DIM, HEAD_DIM)
    eye = jnp.eye(HEADS_PER_GROUP, dtype=s0.dtype)
    bd = jnp.einsum('bgikv,ij->bgikjv', st, eye)
    return bd.reshape(b, N_GROUPS, GROUP, GROUP)


def _blockdiag_to_state(bd):
    b = bd.shape[0]
    x = bd.reshape(b, N_GROUPS, HEADS_PER_GROUP, HEAD_DIM, HEADS_PER_GROUP, HEAD_DIM)
    d = jnp.diagonal(x, axis1=2, axis2=4)
    d = jnp.moveaxis(d, -1, 2)
    return jnp.swapaxes(d, -1, -2).reshape(b, H_B, HEAD_DIM, HEAD_DIM)


def kernel(x_prompt, x_sample, state_rwkv, c, c_ctx, w_ada, b_ada, norm1_g, norm2_g, w_in, conv_w, w_out_a, w0, w_up, a0, a_up, g_up, k_k, k_a, r_k, ln_g, ln_b, w_o, w_mix, w_q, sub_keys, u_tab, v_tab, final_g):
    n_ctx_seq, ctx_len, _ = x_prompt.shape
    n_lat_seq, lat_len, _ = x_sample.shape
    depth = w_in.shape[0]
    n_ctx_tok = n_ctx_seq * ctx_len
    assert depth == 1, "the final norm is fused into the last stage of a single layer"
    assert ctx_len % SCAN_BLOCK == 0 and lat_len % 512 == 0 and n_ctx_tok % 512 == 0
    assert lat_len % (GRID_W * (CONV_TILE // GRID_W)) == 0

    x_all = jnp.concatenate([x_prompt.reshape(n_ctx_tok, D_MODEL),
                             x_sample.reshape(n_lat_seq * lat_len, D_MODEL)], axis=0)
    mod_rows = 16
    cc = jnp.zeros((mod_rows, D_MODEL), F32).at[0].set(c_ctx).at[1:1 + n_lat_seq].set(c)
    new_states = []
    for l in range(depth):
        w_cat, w_up_bf, a_up_bf = _layer_weights(w_in[l], w_up[l], a_up[l])
        mod = _stage_mod(cc, w_ada[l].astype(BF16), b_ada[l][None, :])
        mod4 = mod.reshape(mod_rows, 6, 1, D_MODEL)
        u = _stage_inproj(x_all, mod4, norm1_g[l][None, :], w_cat, n_ctx_tok, lat_len)
        y_a = _stage_conv(u, conv_w[l], w_out_a[l].astype(BF16), n_ctx_tok, lat_len)
        lw_f, lw_b, al_f, al_b = _stage_rwkv_pre(u, w0[l], w_up_bf, a0[l], a_up_bf)
        kk2, ka2, rk2 = k_k[l][None, :], k_a[l][None, :], r_k[l].reshape(1, D_B)
        seq = (n_ctx_seq, ctx_len, n_lat_seq, lat_len)
        o_f, bon_f, sfin_f = _stage_scan(False, u, lw_f, al_f, _state_to_blockdiag(state_rwkv[:, l, 0]),
                                         kk2, ka2, rk2, None, *seq)
        o, bon, sfin_b = _stage_scan(True, u, lw_b, al_b, _state_to_blockdiag(state_rwkv[:, l, 1]),
                                     kk2, ka2, rk2, (o_f, bon_f), *seq)
        new_states.append(jnp.stack([_blockdiag_to_state(sfin_f[:n_ctx_seq]),
                                     _blockdiag_to_state(sfin_b[:n_ctx_seq])], axis=1))
        y_b = _stage_post(o, bon, u, ln_g[l][None, :], ln_b[l][None, :], g_up[l].astype(BF16), w_o[l].astype(BF16))
        x1, h2 = _stage_merge(y_a, y_b, u, x_all, mod4, norm2_g[l][None, :], w_mix[l].astype(BF16),
                              n_ctx_tok, lat_len)
        keys = sub_keys[l].reshape(2 * PEER_HEADS, N_KEYS, HALF_KEY).astype(BF16)
        eid_t, gate_t = _stage_topk(h2, w_q[l].T.astype(BF16), keys)
        tab = jnp.concatenate([u_tab[l], v_tab[l]], axis=1)[:, None, :]
        x_all = _stage_gather(eid_t.T, gate_t.T, h2, x1, mod4, final_g[None, :], tab, n_ctx_tok, lat_len)
    y_prompt = x_all[:n_ctx_tok].reshape(x_prompt.shape)
    y_sample = x_all[n_ctx_tok:].reshape(x_sample.shape)
    return (y_prompt, y_sample, jnp.stack(new_states, axis=1))
```

```python
import functools
import math

import jax
import jax.numpy as jnp
from jax import lax
from jax.experimental import pallas as pl
from jax.experimental.pallas import tpu as pltpu

F32 = jnp.float32
BF16 = jnp.bfloat16

D_MODEL = 2048
GRID_W = 64
D_A = 1024
H_B = 32
HEAD_DIM = 64
D_B = H_B * HEAD_DIM
LORA_W = 96
LORA_A = 96
LORA_G = 256
PEER_HEADS = 8
N_KEYS = 128
HALF_KEY = 128
PEER_TOPK = 16
RMS_EPS = 1e-6
GN_EPS = 64e-5

LANE = 128
CHUNK = 64
SCAN_BLOCK = 256
GROUP = 256
HEADS_PER_GROUP = GROUP // HEAD_DIM
N_GROUPS = D_B // GROUP
LORA_PAD = 128
EXPERTS_PER_TOKEN = PEER_HEADS * PEER_TOPK
GATHER_TOKENS = 8
VMEM_LIMIT = 56 * 1024 * 1024

COL_R, COL_K, COL_V, COL_GA, COL_GB = 0, 2048, 4096, 6144, 8192
COL_UB, COL_UC, COL_UX, COL_LORA = 10240, 11264, 12288, 13312
D_IN_PAD = 14336
LORA_BLOCK = 1024


def _cparams(sem, vmem=VMEM_LIMIT):
    return pltpu.CompilerParams(dimension_semantics=sem, vmem_limit_bytes=vmem)


def _dot(a, b):
    return jnp.dot(a, b, preferred_element_type=F32)


def _dot_nt(a, b):
    return lax.dot_general(a, b, (((1,), (1,)), ((), ())), preferred_element_type=F32)


def _dot_tn(a, b):
    return lax.dot_general(a, b, (((0,), (0,)), ((), ())), preferred_element_type=F32)


def _split(x, parts):
    out = []
    for _ in range(parts):
        p = x.astype(BF16)
        out.append(p)
        x = x - p.astype(F32)
    return out


def _head_ones():
    r = lax.broadcasted_iota(jnp.int32, (GROUP, GROUP), 0) // HEAD_DIM
    c = lax.broadcasted_iota(jnp.int32, (GROUP, GROUP), 1) // HEAD_DIM
    return r == c


def _headsum(x, ones_bf):
    outs = []
    for g in range(x.shape[1] // GROUP):
        xs = x[:, g * GROUP:(g + 1) * GROUP]
        hi, lo = _split(xs, 2)
        outs.append(_dot(hi, ones_bf) + _dot(lo, ones_bf))
    return outs[0] if len(outs) == 1 else jnp.concatenate(outs, axis=1)


def _rms(x):
    return x * lax.rsqrt(jnp.mean(x * x, axis=-1, keepdims=True) + RMS_EPS)


def _mod_body(c_ref, w_ref, b_ref, o_ref):
    c = c_ref[...]
    s = (c * jax.nn.sigmoid(c)).astype(BF16)
    o_ref[...] = _dot(s, w_ref[...]) + b_ref[...]


def _stage_mod(cc, w_ada_bf, b_ada):
    rows, d = cc.shape
    n = w_ada_bf.shape[1]
    tn = 2048
    return pl.pallas_call(
        _mod_body,
        out_shape=jax.ShapeDtypeStruct((rows, n), F32),
        grid=(n // tn,),
        in_specs=[pl.BlockSpec((rows, d), lambda j: (0, 0)),
                  pl.BlockSpec((d, tn), lambda j: (0, j)),
                  pl.BlockSpec((1, tn), lambda j: (0, j))],
        out_specs=pl.BlockSpec((rows, tn), lambda j: (0, j)),
        compiler_params=_cparams(("arbitrary",)),
        name="adaln_mod",
    )(cc, w_ada_bf, b_ada)


def _mod_spec(which, tm, n_ctx_tok, lat_len):
    nct = n_ctx_tok // tm
    per = lat_len // tm

    def imap(i, *_):
        row = jnp.where(i < nct, 0, 1 + (i - nct) // per)
        return (row, which, 0, 0)

    return pl.BlockSpec((None, None, 1, D_MODEL), imap)


def _inproj_body(x_ref, sh_ref, sc_ref, g_ref, w_ref, o_ref, h_ref):
    @pl.when(pl.program_id(1) == 0)
    def _():
        h = _rms(x_ref[...]) * g_ref[...] * (1.0 + sc_ref[...]) + sh_ref[...]
        h_ref[...] = h.astype(BF16)

    o_ref[...] = _dot(h_ref[...], w_ref[...])


def _stage_inproj(x_all, mod4, norm1_g, w_cat, n_ctx_tok, lat_len):
    t = x_all.shape[0]
    tm, tn = 512, 2048
    return pl.pallas_call(
        _inproj_body,
        out_shape=jax.ShapeDtypeStruct((t, D_IN_PAD), F32),
        grid=(t // tm, D_IN_PAD // tn),
        in_specs=[pl.BlockSpec((tm, D_MODEL), lambda i, j: (i, 0)),
                  _mod_spec(0, tm, n_ctx_tok, lat_len),
                  _mod_spec(1, tm, n_ctx_tok, lat_len),
                  pl.BlockSpec((1, D_MODEL), lambda i, j: (0, 0)),
                  pl.BlockSpec((D_MODEL, tn), lambda i, j: (0, j))],
        out_specs=pl.BlockSpec((tm, tn), lambda i, j: (i, j)),
        scratch_shapes=[pltpu.VMEM((tm, D_MODEL), BF16)],
        compiler_params=_cparams(("parallel", "arbitrary")),
        name="norm1_inproj",
    )(x_all, mod4, mod4, norm1_g, w_cat)


CONV_TILE = 256


def _conv_body(n_ctx_tiles, tiles_per_lat, ub_ref, uc_ref, ux_ref, pc_ref, px_ref, nc_ref, nx_ref,
               cw_ref, wo_ref, y_ref):
    i = pl.program_id(0)
    is_lat = i >= n_ctx_tiles
    jl = (i - n_ctx_tiles) % tiles_per_lat
    half = D_A // 2
    z = uc_ref[...] * ux_ref[...]
    cw = cw_ref[...]
    row = lax.broadcasted_iota(jnp.int32, (CONV_TILE, 1), 0)
    pmask = jnp.where(is_lat, GRID_W - 1, CONV_TILE - 1)
    keep_p = (row & pmask) != 0
    keep_n = (row & pmask) != pmask
    zp = jnp.where(keep_p, pltpu.roll(z, 1, 0), 0.0)
    zn = jnp.where(keep_n, pltpu.roll(z, CONV_TILE - 1, 0), 0.0)
    conv_h = cw[0:1] * zp + cw[1:2] * z + cw[2:3] * zn
    zv = z[:, half:]
    hp = jnp.where(jl != 0, pc_ref[...] * px_ref[...], 0.0)
    hn = jnp.where(jl != tiles_per_lat - 1, nc_ref[...] * nx_ref[...], 0.0)
    vp = jnp.concatenate([hp, zv[:CONV_TILE - GRID_W]], axis=0)
    vn = jnp.concatenate([zv[GRID_W:], hn], axis=0)
    conv_v = cw[0:1, half:] * vp + cw[1:2, half:] * zv + cw[2:3, half:] * vn
    zc = jnp.concatenate([conv_h[:, :half], jnp.where(is_lat, conv_v, conv_h[:, half:])], axis=1)
    y_ref[...] = _dot((ub_ref[...] * zc).astype(BF16), wo_ref[...])


def _stage_conv(u, conv_w, w_out_a_bf, n_ctx_tok, lat_len):
    t = u.shape[0]
    tm = CONV_TILE
    n_ctx_tiles = n_ctx_tok // tm
    tiles_per_lat = lat_len // tm
    rpt = tm // GRID_W
    last_halo = t // GRID_W - 1
    half = D_A // 2
    cb = lambda col: col // D_A
    hb = lambda col: (col + half) // half
    body = functools.partial(_conv_body, n_ctx_tiles, tiles_per_lat)
    prev_map = lambda col: (lambda i: (jnp.maximum(i * rpt - 1, 0), hb(col)))
    next_map = lambda col: (lambda i: (jnp.minimum(i * rpt + rpt, last_halo), hb(col)))
    return pl.pallas_call(
        body,
        out_shape=jax.ShapeDtypeStruct((t, D_MODEL), F32),
        grid=(t // tm,),
        in_specs=[pl.BlockSpec((tm, D_A), lambda i: (i, cb(COL_UB))),
                  pl.BlockSpec((tm, D_A), lambda i: (i, cb(COL_UC))),
                  pl.BlockSpec((tm, D_A), lambda i: (i, cb(COL_UX))),
                  pl.BlockSpec((GRID_W, half), prev_map(COL_UC)),
                  pl.BlockSpec((GRID_W, half), prev_map(COL_UX)),
                  pl.BlockSpec((GRID_W, half), next_map(COL_UC)),
                  pl.BlockSpec((GRID_W, half), next_map(COL_UX)),
                  pl.BlockSpec((3, D_A), lambda i: (0, 0)),
                  pl.BlockSpec((D_A, D_MODEL), lambda i: (0, 0))],
        out_specs=pl.BlockSpec((tm, D_MODEL), lambda i: (i, 0)),
        compiler_params=_cparams(("parallel",)),
        name="short_conv",
    )(u, u, u, u, u, u, u, conv_w, w_out_a_bf)


def _rwkv_pre_body(lora_ref, w0_ref, wup_ref, a0_ref, aup_ref, lwf_ref, lwb_ref, alf_ref, alb_ref):
    lo = lora_ref[...]
    lw_refs = (lwf_ref, lwb_ref)
    al_refs = (alf_ref, alb_ref)
    for d in range(2):
        wd = lo[:, d * LORA_PAD:(d + 1) * LORA_PAD]
        ad = lo[:, (2 + d) * LORA_PAD:(3 + d) * LORA_PAD]
        xw = w0_ref[d:d + 1, :] + _dot(jnp.tanh(wd).astype(BF16), wup_ref[d])
        lw_refs[d][...] = -math.exp(-0.5) * jax.nn.sigmoid(xw)
        al_refs[d][...] = jax.nn.sigmoid(a0_ref[d:d + 1, :] + _dot(ad.astype(BF16), aup_ref[d]))


def _stage_rwkv_pre(u, w0, w_up_bf, a0, a_up_bf):
    t = u.shape[0]
    tm = 512
    out = jax.ShapeDtypeStruct((t, D_B), F32)
    ospec = pl.BlockSpec((tm, D_B), lambda i: (i, 0))
    return pl.pallas_call(
        _rwkv_pre_body,
        out_shape=(out, out, out, out),
        grid=(t // tm,),
        in_specs=[pl.BlockSpec((tm, LORA_BLOCK), lambda i: (i, COL_LORA // LORA_BLOCK)),
                  pl.BlockSpec((2, D_B), lambda i: (0, 0)),
                  pl.BlockSpec((2, LORA_PAD, D_B), lambda i: (0, 0, 0)),
                  pl.BlockSpec((2, D_B), lambda i: (0, 0)),
                  pl.BlockSpec((2, LORA_PAD, D_B), lambda i: (0, 0, 0))],
        out_specs=(ospec, ospec, ospec, ospec),
        compiler_params=_cparams(("parallel",)),
        name="rwkv_pre",
    )(u, w0, w_up_bf, a0, a_up_bf)


def _scan_phase1(reverse, units, kk_p, ka_p, rk_p, c):
    ones_bf, bdmask, tri_bf, strict4, incl4, eye4, ones_tn = c
    n = len(units)
    rng = range(n)

    def bd(x_bf):
        return jnp.concatenate([x_bf] * HEADS_PER_GROUP, axis=0) * ones_bf

    def bd2(x, y):
        return jnp.concatenate([bd(x.astype(BF16)), bd(y.astype(BF16))], axis=1)

    r = [u[0] for u in units]
    k = [u[1] for u in units]
    v = [u[2] for u in units]
    lw = [u[3] for u in units]
    al = [u[4] for u in units]
    kraw = [k[i] * kk_p for i in rng]
    ss = [_headsum(kraw[i] * kraw[i], ones_bf) for i in rng]
    kkn = [kraw[i] * lax.rsqrt(ss[i] + 1e-12) for i in rng]
    kd = [k[i] * (1.0 + (al[i] - 1.0) * ka_p) for i in rng]
    b = [kkn[i] * al[i] for i in rng]
    bonus = [_headsum(r[i] * kd[i] * rk_p, ones_bf) * v[i] for i in rng]

    parts = [_split(lw[i], 3) for i in rng]
    lc = [sum(_dot(tri_bf, p) for p in parts[i]) for i in rng]
    decay_col = [jnp.exp(sum(_dot_tn(p, ones_tn) for p in parts[i])) for i in rng]
    ltot = [(lc[i][0:1] if reverse else lc[i][CHUNK - 1:CHUNK]) for i in rng]
    rt = [r[i] * jnp.exp(lc[i]) for i in rng]
    at = [-kkn[i] * jnp.exp(lc[i] - lw[i]) for i in rng]
    einv = [jnp.exp(-lc[i]) for i in rng]
    bh = [b[i] * einv[i] for i in rng]
    kh = [kd[i] * einv[i] for i in rng]
    etail = [jnp.exp(ltot[i] - lc[i]) for i in rng]
    bt = [(b[i] * etail[i]).astype(BF16) for i in rng]
    kt = [(kd[i] * etail[i]).astype(BF16) for i in rng]
    v_bf = [v[i].astype(BF16) for i in rng]

    lhs1 = [jnp.concatenate([at[i], rt[i]], axis=0).astype(BF16) for i in rng]
    ab = [_dot_nt(lhs1[i], bd(bh[i].astype(BF16))) for i in rng]
    ak = [_dot_nt(lhs1[i], bd(kh[i].astype(BF16))) for i in rng]
    a_ab = [jnp.where(strict4, ab[i][:CHUNK], 0.0) for i in rng]
    a_rb = [jnp.where(incl4, ab[i][CHUNK:], 0.0).astype(BF16) for i in rng]
    a_ak = [jnp.where(strict4, ak[i][:CHUNK], 0.0).astype(BF16) for i in rng]
    a_rk = [jnp.where(incl4, ak[i][CHUNK:], 0.0).astype(BF16) for i in rng]

    x = a_ab
    tinv = [eye4 + x[i] for i in rng]
    x = [_dot(x[i].astype(BF16), bd(x[i].astype(BF16))) for i in rng]
    for _ in range(4):
        y = [_dot(jnp.concatenate([x[i], tinv[i]], axis=0).astype(BF16), bd(x[i].astype(BF16))) for i in rng]
        x = [y[i][:CHUNK] for i in rng]
        tinv = [tinv[i] + y[i][CHUNK:] for i in rng]
    tinv = [(tinv[i] + _dot(tinv[i].astype(BF16), bd(x[i].astype(BF16)))).astype(BF16) for i in rng]

    bdv = [bd(v_bf[i]) for i in rng]
    akv = [_dot(a_ak[i], bdv[i]) for i in rng]
    wu = [_dot(tinv[i], bd2(at[i], akv[i])) for i in rng]
    wu_bf = [wu[i].astype(BF16) for i in rng]
    qo = [_dot(a_rb[i], bd2(wu[i][:, :GROUP], wu[i][:, GROUP:])) for i in rng]
    q = [(rt[i] + qo[i][:, :GROUP]).astype(BF16) for i in rng]
    o0 = [qo[i][:, GROUP:] + _dot(a_rk[i], bdv[i]) for i in rng]
    mg = [_dot_tn(bt[i], wu_bf[i]) for i in rng]
    kv = [_dot_tn(kt[i], v_bf[i]) for i in rng]
    m_bd = [jnp.where(bdmask, mg[i][:, :GROUP], 0.0).astype(BF16) for i in rng]
    g_bd = [jnp.where(bdmask, mg[i][:, GROUP:] + kv[i], 0.0) for i in rng]
    return [(q[i], o0[i], m_bd[i], g_bd[i], decay_col[i], bonus[i]) for i in rng]


def _scan_body(reverse, has_prev, n_ctx_blk, blk_per_ctx, blk_per_lat, n_blk, *refs):
    if has_prev:
        (r_ref, k_ref, v_ref, lw_ref, al_ref, s0_ref, kk_ref, ka_ref, rk_ref, op_ref, bp_ref,
         o_ref, bon_ref, sfin_ref, h_ref) = refs
    else:
        (r_ref, k_ref, v_ref, lw_ref, al_ref, s0_ref, kk_ref, ka_ref, rk_ref,
         o_ref, bon_ref, sfin_ref, h_ref) = refs
        op_ref = bp_ref = None
    s = pl.program_id(1)
    j = (n_blk - 1 - s) if reverse else s
    is_ctx = j < n_ctx_blk
    first_c = (blk_per_ctx - 1) if reverse else 0
    first_l = (blk_per_lat - 1) if reverse else 0
    start = jnp.where(is_ctx, (j % blk_per_ctx) == first_c, ((j - n_ctx_blk) % blk_per_lat) == first_l)

    @pl.when(start)
    def _():
        h_ref[...] = jnp.where(is_ctx, 0.0, s0_ref[...])

    ones_mask = _head_ones()
    ones_bf = ones_mask.astype(F32).astype(BF16)
    t_i = lax.broadcasted_iota(jnp.int32, (CHUNK, CHUNK), 0)
    s_i = lax.broadcasted_iota(jnp.int32, (CHUNK, CHUNK), 1)
    tri = (s_i >= t_i) if reverse else (s_i <= t_i)
    tri_bf = tri.astype(F32).astype(BF16)
    t4 = lax.broadcasted_iota(jnp.int32, (CHUNK, GROUP), 0)
    s4 = lax.broadcasted_iota(jnp.int32, (CHUNK, GROUP), 1) % HEAD_DIM
    incl4 = (s4 >= t4) if reverse else (s4 <= t4)
    strict4 = (s4 > t4) if reverse else (s4 < t4)
    eye4 = (s4 == t4).astype(F32)
    ones_tn = jnp.ones((CHUNK, GROUP), BF16)
    consts = (ones_bf, ones_mask, tri_bf, strict4, incl4, eye4, ones_tn)

    n_chunks = SCAN_BLOCK // CHUNK
    order = list(range(n_chunks - 1, -1, -1) if reverse else range(n_chunks))
    sls = [slice(cidx * CHUNK, (cidx + 1) * CHUNK) for cidx in order]
    units = [(r_ref[sl, :], k_ref[sl, :], v_ref[sl, :], lw_ref[sl, :], al_ref[sl, :]) for sl in sls]
    pre = _scan_phase1(reverse, units, kk_ref[...], ka_ref[...], rk_ref[...], consts)

    h_state = h_ref[...]
    for sl, (q, o0, m_bd, g_bd, decay_col, bonus) in zip(sls, pre):
        h_bf = h_state.astype(BF16)
        o = _dot(q, h_bf) + o0
        h_state = decay_col * h_state + _dot(m_bd, h_bf) + g_bd
        if has_prev:
            o = o + op_ref[sl, :]
            bonus = bonus + bp_ref[sl, :]
        o_ref[sl, :] = o
        bon_ref[sl, :] = bonus
    h_ref[...] = h_state
    sfin_ref[...] = h_state


def _stage_scan(reverse, u, lw, al, s0_bd, k_k, k_a, r_k, prev, n_ctx_seq, ctx_len, n_lat_seq, lat_len):
    t = u.shape[0]
    blk_per_ctx = ctx_len // SCAN_BLOCK
    blk_per_lat = lat_len // SCAN_BLOCK
    n_ctx_blk = n_ctx_seq * blk_per_ctx
    n_blk = t // SCAN_BLOCK
    has_prev = prev is not None

    def jmap(s):
        return (n_blk - 1 - s) if reverse else s

    def tok(colbase):
        return pl.BlockSpec((SCAN_BLOCK, GROUP), lambda g, s: (jmap(s), colbase // GROUP + g))

    def s0_map(g, s):
        j = jmap(s)
        b = jnp.clip((j - n_ctx_blk) // blk_per_lat, 0, n_lat_seq - 1)
        return (b, g, 0, 0)

    def sfin_map(g, s):
        j = jmap(s)
        b = jnp.where(j < n_ctx_blk, j // blk_per_ctx, n_ctx_seq)
        return (b, g, 0, 0)

    par = pl.BlockSpec((1, GROUP), lambda g, s: (0, g))
    in_specs = [tok(COL_R), tok(COL_K), tok(COL_V), tok(0), tok(0),
                pl.BlockSpec((None, None, GROUP, GROUP), s0_map), par, par, par]
    args = [u, u, u, lw, al, s0_bd, k_k, k_a, r_k]
    if has_prev:
        in_specs += [tok(0), tok(0)]
        args += list(prev)
    out = jax.ShapeDtypeStruct((t, D_B), F32)
    body = functools.partial(_scan_body, reverse, has_prev, n_ctx_blk, blk_per_ctx, blk_per_lat, n_blk)
    return pl.pallas_call(
        body,
        out_shape=(out, out, jax.ShapeDtypeStruct((n_ctx_seq + 1, N_GROUPS, GROUP, GROUP), F32)),
        grid=(N_GROUPS, n_blk),
        in_specs=in_specs,
        out_specs=(tok(0), tok(0), pl.BlockSpec((None, None, GROUP, GROUP), sfin_map)),
        scratch_shapes=[pltpu.VMEM((GROUP, GROUP), F32)],
        compiler_params=_cparams(("parallel", "arbitrary")),
        name="rwkv_scan_bwd" if reverse else "rwkv_scan_fwd",
    )(*args)


def _post_body(o_ref, bon_ref, lora_ref, lng_ref, lnb_ref, gup_ref, wo_ref, y_ref):
    ones_bf = _head_ones().astype(F32).astype(BF16)
    o = o_ref[...]
    mu = _headsum(o, ones_bf) * (1.0 / HEAD_DIM)
    d = o - mu
    var = _headsum(d * d, ones_bf) * (1.0 / HEAD_DIM)
    on = d * lax.rsqrt(var + GN_EPS) * lng_ref[...] + lnb_ref[...] + bon_ref[...]
    gd = lora_ref[:, 4 * LORA_PAD:4 * LORA_PAD + LORA_G]
    g = _dot(jax.nn.sigmoid(gd).astype(BF16), gup_ref[...])
    y_ref[...] = _dot((on * g).astype(BF16), wo_ref[...])


def _stage_post(o, bon, u, ln_g, ln_b, g_up_bf, w_o_bf):
    t = o.shape[0]
    tm = 256
    return pl.pallas_call(
        _post_body,
        out_shape=jax.ShapeDtypeStruct((t, D_MODEL), F32),
        grid=(t // tm,),
        in_specs=[pl.BlockSpec((tm, D_B), lambda i: (i, 0)),
                  pl.BlockSpec((tm, D_B), lambda i: (i, 0)),
                  pl.BlockSpec((tm, LORA_BLOCK), lambda i: (i, COL_LORA // LORA_BLOCK)),
                  pl.BlockSpec((1, D_B), lambda i: (0, 0)),
                  pl.BlockSpec((1, D_B), lambda i: (0, 0)),
                  pl.BlockSpec((LORA_G, D_B), lambda i: (0, 0)),
                  pl.BlockSpec((D_B, D_MODEL), lambda i: (0, 0))],
        out_specs=pl.BlockSpec((tm, D_MODEL), lambda i: (i, 0)),
        compiler_params=_cparams(("parallel",)),
        name="rwkv_post",
    )(o, bon, u, ln_g, ln_b, g_up_bf, w_o_bf)


def _merge_body(ya_ref, yb_ref, ga_ref, gb_ref, x_ref, gt_ref, sh_ref, sc_ref, g_ref, w_ref, x1_ref, h2_ref):
    m = jax.nn.sigmoid(ga_ref[...]) * ya_ref[...] + jax.nn.sigmoid(gb_ref[...]) * yb_ref[...]
    x1 = x_ref[...] + gt_ref[...] * _dot(m.astype(BF16), w_ref[...])
    x1_ref[...] = x1
    h2_ref[...] = _rms(x1) * g_ref[...] * (1.0 + sc_ref[...]) + sh_ref[...]


def _stage_merge(y_a, y_b, u, x_all, mod4, norm2_g, w_mix_bf, n_ctx_tok, lat_len):
    t = x_all.shape[0]
    tm = 256
    full = lambda col: pl.BlockSpec((tm, D_MODEL), lambda i: (i, col // D_MODEL))
    out = jax.ShapeDtypeStruct((t, D_MODEL), F32)
    return pl.pallas_call(
        _merge_body,
        out_shape=(out, out),
        grid=(t // tm,),
        in_specs=[full(0), full(0), full(COL_GA), full(COL_GB), full(0),
                  _mod_spec(2, tm, n_ctx_tok, lat_len),
                  _mod_spec(3, tm, n_ctx_tok, lat_len),
                  _mod_spec(4, tm, n_ctx_tok, lat_len),
                  pl.BlockSpec((1, D_MODEL), lambda i: (0, 0)),
                  pl.BlockSpec((D_MODEL, D_MODEL), lambda i: (0, 0))],
        out_specs=(full(0), full(0)),
        compiler_params=_cparams(("parallel",)),
        name="merge_norm2",
    )(y_a, y_b, u, u, x_all, mod4, mod4, mod4, norm2_g, w_mix_bf)


TOPK_TOKENS = 256


def _extract_topk(s, pos, limit, payload=None):
    vals, outs = [], []
    for _ in range(PEER_TOPK):
        m = jnp.max(s, axis=0, keepdims=True)
        pm = jnp.min(jnp.where(s == m, pos, limit), axis=0, keepdims=True)
        sel = pos == pm
        vals.append(m)
        if payload is None:
            outs.append(pm)
        else:
            outs.append(jnp.max(jnp.where(sel, payload, -1), axis=0, keepdims=True))
        s = jnp.where(sel, -jnp.inf, s)
    return jnp.concatenate(vals, axis=0), jnp.concatenate(outs, axis=0)


def _topk_body(h_ref, wq_ref, keys_ref, eid_ref, gate_ref, q_ref):
    q_ref[...] = _dot_nt(wq_ref[...], h_ref[...].astype(BF16))
    kpos = lax.broadcasted_iota(jnp.int32, (N_KEYS, TOPK_TOKENS), 0)
    cpos = lax.broadcasted_iota(jnp.int32, (PEER_TOPK * PEER_TOPK, TOPK_TOKENS), 0)

    def head(hd, carry):
        tops = []
        for p in range(2):
            off = pl.multiple_of(hd * (2 * HALF_KEY) + p * HALF_KEY, HALF_KEY)
            qs = q_ref[pl.ds(off, HALF_KEY), :].astype(BF16)
            s = _dot(keys_ref[p * PEER_HEADS + hd], qs)
            tops.append(_extract_topk(s, kpos, N_KEYS))
        (s1, i1), (s2, i2) = tops
        cand = jnp.concatenate([s1[i:i + 1] + s2 for i in range(PEER_TOPK)], axis=0)
        ids = jnp.concatenate([i1[i:i + 1] * N_KEYS + i2 for i in range(PEER_TOPK)], axis=0)
        tv, te = _extract_topk(cand, cpos, PEER_TOPK * PEER_TOPK, payload=ids)
        ex = jnp.exp(tv - tv[0:1])
        gate = ex / jnp.sum(ex, axis=0, keepdims=True)
        row = pl.multiple_of(hd * PEER_TOPK, PEER_TOPK)
        eid_ref[pl.ds(row, PEER_TOPK), :] = te
        gate_ref[pl.ds(row, PEER_TOPK), :] = gate
        return carry

    lax.fori_loop(0, PEER_HEADS, head, 0)


def _stage_topk(h2, w_q_t_bf, keys_bf):
    t = h2.shape[0]
    tb = TOPK_TOKENS
    return pl.pallas_call(
        _topk_body,
        out_shape=(jax.ShapeDtypeStruct((EXPERTS_PER_TOKEN, t), jnp.int32),
                   jax.ShapeDtypeStruct((EXPERTS_PER_TOKEN, t), F32)),
        grid=(t // tb,),
        in_specs=[pl.BlockSpec((tb, D_MODEL), lambda i: (i, 0)),
                  pl.BlockSpec((PEER_HEADS * 2 * HALF_KEY, D_MODEL), lambda i: (0, 0)),
                  pl.BlockSpec((2 * PEER_HEADS, N_KEYS, HALF_KEY), lambda i: (0, 0, 0))],
        out_specs=(pl.BlockSpec((EXPERTS_PER_TOKEN, tb), lambda i: (0, i)),
                   pl.BlockSpec((EXPERTS_PER_TOKEN, tb), lambda i: (0, i))),
        scratch_shapes=[pltpu.VMEM((PEER_HEADS * 2 * HALF_KEY, tb), F32)],
        compiler_params=_cparams(("parallel",)),
        name="peer_topk",
    )(h2, w_q_t_bf, keys_bf)


def _gather_copy(tab_ref, buf, sem, row, expert):
    return pltpu.make_async_copy(tab_ref.at[expert], buf.at[pl.ds(row, 1), :], sem)


def _gather_issue(tab_ref, id_ref, row0, buf, sem):
    for t in range(GATHER_TOKENS):
        for e in range(EXPERTS_PER_TOKEN):
            _gather_copy(tab_ref, buf, sem, t * EXPERTS_PER_TOKEN + e, id_ref[row0 + t, e]).start()


def _gather_wait(tab_ref, buf, sem):
    for r in range(GATHER_TOKENS * EXPERTS_PER_TOKEN):
        _gather_copy(tab_ref, buf, sem, r, 0).wait()


def _mix_tokens(buf, gate, h, eye):
    ne = EXPERTS_PER_TOKEN
    n_tiles = D_MODEL // LANE
    outs = []
    for t in range(GATHER_TOKENS):
        rows = slice(t * ne, (t + 1) * ne)
        acc = jnp.zeros((ne, LANE), F32)
        for j in range(n_tiles):
            w = buf[rows, j * LANE:(j + 1) * LANE]
            u = lax.bitcast_convert_type(w << 16, F32)
            acc = acc + u * h[t:t + 1, j * LANE:(j + 1) * LANE]
        pre = jnp.sum(acc, axis=-1, keepdims=True)
        gcol = jnp.sum(jnp.where(eye, gate[t:t + 1, :], 0.0), axis=-1, keepdims=True)
        coef = gcol * jax.nn.gelu(pre, approximate=True)
        cb = jnp.broadcast_to(coef, (ne, LANE))
        tiles = []
        for j in range(n_tiles):
            w = buf[rows, j * LANE:(j + 1) * LANE]
            v = lax.bitcast_convert_type(w & jnp.uint32(0xFFFF0000), F32)
            tiles.append(jnp.sum(cb * v, axis=0, keepdims=True))
        outs.append(jnp.concatenate(tiles, axis=1))
    return jnp.concatenate(outs, axis=0)


def _gather_body(ids_ref, idn_ref, gate_ref, h_ref, x_ref, gt_ref, fg_ref, tab_ref, o_ref,
                 buf0, buf1, sem):
    i = pl.program_id(0)
    n = pl.num_programs(0)
    tg = GATHER_TOKENS
    eye = (lax.broadcasted_iota(jnp.int32, (EXPERTS_PER_TOKEN, EXPERTS_PER_TOKEN), 0)
           == lax.broadcasted_iota(jnp.int32, (EXPERTS_PER_TOKEN, EXPERTS_PER_TOKEN), 1))

    @pl.when(i == 0)
    def _():
        _gather_issue(tab_ref, ids_ref, 0, buf0, sem.at[0])

    def finish(mix, lo):
        x2 = x_ref[lo:lo + tg, :] + gt_ref[...] * mix
        o_ref[lo:lo + tg, :] = _rms(x2) * fg_ref[...]

    _gather_wait(tab_ref, buf0, sem.at[0])
    _gather_issue(tab_ref, ids_ref, tg, buf1, sem.at[1])
    finish(_mix_tokens(buf0, gate_ref[0:tg, :], h_ref[0:tg, :], eye), 0)
    _gather_wait(tab_ref, buf1, sem.at[1])
    _gather_issue(tab_ref, idn_ref, 0, buf0, sem.at[0])
    finish(_mix_tokens(buf1, gate_ref[tg:2 * tg, :], h_ref[tg:2 * tg, :], eye), tg)

    @pl.when(i == n - 1)
    def _():
        _gather_wait(tab_ref, buf0, sem.at[0])


def _stage_gather(eid, gate, h2, x1, mod4, final_g, tab, n_ctx_tok, lat_len):
    t = h2.shape[0]
    tg = GATHER_TOKENS
    ts = 2 * tg
    n = t // ts
    tokspec = lambda w: pl.BlockSpec((ts, w), lambda i: (i, 0))
    rows = tg * EXPERTS_PER_TOKEN
    return pl.pallas_call(
        _gather_body,
        out_shape=jax.ShapeDtypeStruct((t, D_MODEL), F32),
        grid=(n,),
        in_specs=[pl.BlockSpec((ts, EXPERTS_PER_TOKEN), lambda i: (i, 0), memory_space=pltpu.SMEM),
                  pl.BlockSpec((tg, EXPERTS_PER_TOKEN), lambda i: (jnp.minimum(2 * i + 2, 2 * n - 1), 0),
                               memory_space=pltpu.SMEM),
                  tokspec(EXPERTS_PER_TOKEN), tokspec(D_MODEL), tokspec(D_MODEL),
                  _mod_spec(5, ts, n_ctx_tok, lat_len),
                  pl.BlockSpec((1, D_MODEL), lambda i: (0, 0)),
                  pl.BlockSpec(memory_space=pl.ANY)],
        out_specs=tokspec(D_MODEL),
        scratch_shapes=[pltpu.VMEM((rows, D_MODEL), jnp.uint32),
                        pltpu.VMEM((rows, D_MODEL), jnp.uint32),
                        pltpu.SemaphoreType.DMA((2,))],
        compiler_params=_cparams(("arbitrary",)),
        name="peer_gather",
    )(eid, eid, gate, h2, x1, mod4, final_g, tab)


def _pack_expert_tables(u_tab, v_tab):
    lo = lax.bitcast_convert_type(u_tab.astype(BF16), jnp.uint16).astype(jnp.uint32)
    hi = lax.bitcast_convert_type(v_tab.astype(BF16), jnp.uint16).astype(jnp.uint32)
    return (lo | (hi << 16))[:, None, :]


def _pad_rows(w, rows):
    return jnp.pad(w, ((0, 0), (0, rows - w.shape[1]), (0, 0)))


def _layer_weights(w_in, w_up, a_up):
    widths = (D_A, D_A, D_A, D_B, D_B, D_B, LORA_W, LORA_W, LORA_A, LORA_A, LORA_G, D_MODEL, D_MODEL)
    cuts = [0]
    for w in widths:
        cuts.append(cuts[-1] + w)
    seg = [w_in[:, cuts[i]:cuts[i + 1]] for i in range(len(widths))]
    (u_b, u_c, u_x, r, k, v, wd_f, wd_b, ad_f, ad_b, gd, gate_a, gate_b) = seg
    padc = lambda m: jnp.pad(m, ((0, 0), (0, LORA_PAD - m.shape[1])))
    tail = jnp.zeros((w_in.shape[0], LORA_BLOCK - 4 * LORA_PAD - LORA_G), w_in.dtype)
    w_cat = jnp.concatenate([r, k, v, gate_a, gate_b, u_b, u_c, u_x,
                             padc(wd_f), padc(wd_b), padc(ad_f), padc(ad_b), gd, tail], axis=1)
    return w_cat.astype(BF16), _pad_rows(w_up, LORA_PAD).astype(BF16), _pad_rows(a_up, LORA_PAD).astype(BF16)


def _state_to_blockdiag(s0):
    b = s0.shape[0]
    st = jnp.swapaxes(s0, -1, -2).reshape(b, N_GROUPS, HEADS_PER_GROUP, HEAD_DIM, HEAD_DIM)
    eye = jnp.eye(HEADS_PER_GROUP, dtype=s0.dtype)
    bd = jnp.einsum('bgikv,ij->bgikjv', st, eye)
    return bd.reshape(b, N_GROUPS, GROUP, GROUP)


def _blockdiag_to_state(bd):
    b = bd.shape[0]
    x = bd.reshape(b, N_GROUPS, HEADS_PER_GROUP, HEAD_DIM, HEADS_PER_GROUP, HEAD_DIM)
    d = jnp.diagonal(x, axis1=2, axis2=4)
    d = jnp.moveaxis(d, -1, 2)
    return jnp.swapaxes(d, -1, -2).reshape(b, H_B, HEAD_DIM, HEAD_DIM)


def kernel(x_prompt, x_sample, state_rwkv, c, c_ctx, w_ada, b_ada, norm1_g, norm2_g, w_in, conv_w, w_out_a, w0, w_up, a0, a_up, g_up, k_k, k_a, r_k, ln_g, ln_b, w_o, w_mix, w_q, sub_keys, u_tab, v_tab, final_g):
    n_ctx_seq, ctx_len, _ = x_prompt.shape
    n_lat_seq, lat_len, _ = x_sample.shape
    depth = w_in.shape[0]
    n_ctx_tok = n_ctx_seq * ctx_len
    assert depth == 1, "the final norm is fused into the last stage of a single layer"
    assert ctx_len % SCAN_BLOCK == 0 and lat_len % 512 == 0 and n_ctx_tok % 512 == 0
    assert lat_len % (GRID_W * (CONV_TILE // GRID_W)) == 0

    x_all = jnp.concatenate([x_prompt.reshape(n_ctx_tok, D_MODEL),
                             x_sample.reshape(n_lat_seq * lat_len, D_MODEL)], axis=0)
    mod_rows = 16
    cc = jnp.zeros((mod_rows, D_MODEL), F32).at[0].set(c_ctx).at[1:1 + n_lat_seq].set(c)
    new_states = []
    for l in range(depth):
        w_cat, w_up_bf, a_up_bf = _layer_weights(w_in[l], w_up[l], a_up[l])
        mod = _stage_mod(cc, w_ada[l].astype(BF16), b_ada[l][None, :])
        mod4 = mod.reshape(mod_rows, 6, 1, D_MODEL)
        u = _stage_inproj(x_all, mod4, norm1_g[l][None, :], w_cat, n_ctx_tok, lat_len)
        y_a = _stage_conv(u, conv_w[l], w_out_a[l].astype(BF16), n_ctx_tok, lat_len)
        lw_f, lw_b, al_f, al_b = _stage_rwkv_pre(u, w0[l], w_up_bf, a0[l], a_up_bf)
        kk2, ka2, rk2 = k_k[l][None, :], k_a[l][None, :], r_k[l].reshape(1, D_B)
        seq = (n_ctx_seq, ctx_len, n_lat_seq, lat_len)
        o_f, bon_f, sfin_f = _stage_scan(False, u, lw_f, al_f, _state_to_blockdiag(state_rwkv[:, l, 0]),
                                         kk2, ka2, rk2, None, *seq)
        o, bon, sfin_b = _stage_scan(True, u, lw_b, al_b, _state_to_blockdiag(state_rwkv[:, l, 1]),
                                     kk2, ka2, rk2, (o_f, bon_f), *seq)
        new_states.append(jnp.stack([_blockdiag_to_state(sfin_f[:n_ctx_seq]),
                                     _blockdiag_to_state(sfin_b[:n_ctx_seq])], axis=1))
        y_b = _stage_post(o, bon, u, ln_g[l][None, :], ln_b[l][None, :], g_up[l].astype(BF16), w_o[l].astype(BF16))
        x1, h2 = _stage_merge(y_a, y_b, u, x_all, mod4, norm2_g[l][None, :], w_mix[l].astype(BF16),
                              n_ctx_tok, lat_len)
        keys = sub_keys[l].reshape(2 * PEER_HEADS, N_KEYS, HALF_KEY).astype(BF16)
        eid_t, gate_t = _stage_topk(h2, w_q[l].T.astype(BF16), keys)
        tab = _pack_expert_tables(u_tab[l], v_tab[l])
        x_all = _stage_gather(eid_t.T, gate_t.T, h2, x1, mod4, final_g[None, :], tab, n_ctx_tok, lat_len)
    y_prompt = x_all[:n_ctx_tok].reshape(x_prompt.shape)
    y_sample = x_all[n_ctx_tok:].reshape(x_sample.shape)
    return (y_prompt, y_sample, jnp.stack(new_states, axis=1))
```

```python
import functools
import math

import jax
import jax.numpy as jnp
from jax import lax
from jax.experimental import pallas as pl
from jax.experimental.pallas import tpu as pltpu

F32 = jnp.float32
BF16 = jnp.bfloat16

D_MODEL = 2048
GRID_W = 64
D_A = 1024
H_B = 32
HEAD_DIM = 64
D_B = H_B * HEAD_DIM
LORA_W = 96
LORA_A = 96
LORA_G = 256
PEER_HEADS = 8
N_KEYS = 128
HALF_KEY = 128
PEER_TOPK = 16
RMS_EPS = 1e-6
GN_EPS = 64e-5

LANE = 128
CHUNK = 64
SCAN_BLOCK = 256
GROUP = 256
HEADS_PER_GROUP = GROUP // HEAD_DIM
N_GROUPS = D_B // GROUP
LORA_PAD = 128
EXPERTS_PER_TOKEN = PEER_HEADS * PEER_TOPK
GATHER_TOKENS = 8
VMEM_LIMIT = 56 * 1024 * 1024

COL_R, COL_K, COL_V, COL_GA, COL_GB = 0, 2048, 4096, 6144, 8192
COL_UB, COL_UC, COL_UX, COL_LORA = 10240, 11264, 12288, 13312
D_IN_PAD = 14336
LORA_BLOCK = 1024


def _cparams(sem, vmem=VMEM_LIMIT):
    return pltpu.CompilerParams(dimension_semantics=sem, vmem_limit_bytes=vmem)


def _dot(a, b):
    return jnp.dot(a, b, preferred_element_type=F32)


def _dot_nt(a, b):
    return lax.dot_general(a, b, (((1,), (1,)), ((), ())), preferred_element_type=F32)


def _dot_tn(a, b):
    return lax.dot_general(a, b, (((0,), (0,)), ((), ())), preferred_element_type=F32)


def _split(x, parts):
    out = []
    for _ in range(parts):
        p = x.astype(BF16)
        out.append(p)
        x = x - p.astype(F32)
    return out


def _head_ones():
    r = lax.broadcasted_iota(jnp.int32, (GROUP, GROUP), 0) // HEAD_DIM
    c = lax.broadcasted_iota(jnp.int32, (GROUP, GROUP), 1) // HEAD_DIM
    return r == c


def _headsum(x, ones_bf):
    outs = []
    for g in range(x.shape[1] // GROUP):
        xs = x[:, g * GROUP:(g + 1) * GROUP]
        hi, lo = _split(xs, 2)
        outs.append(_dot(hi, ones_bf) + _dot(lo, ones_bf))
    return outs[0] if len(outs) == 1 else jnp.concatenate(outs, axis=1)


def _rms(x):
    return x * lax.rsqrt(jnp.mean(x * x, axis=-1, keepdims=True) + RMS_EPS)


def _mod_body(c_ref, w_ref, b_ref, o_ref):
    c = c_ref[...]
    s = (c * jax.nn.sigmoid(c)).astype(BF16)
    o_ref[...] = _dot(s, w_ref[...]) + b_ref[...]


def _stage_mod(cc, w_ada_bf, b_ada):
    rows, d = cc.shape
    n = w_ada_bf.shape[1]
    tn = 2048
    return pl.pallas_call(
        _mod_body,
        out_shape=jax.ShapeDtypeStruct((rows, n), F32),
        grid=(n // tn,),
        in_specs=[pl.BlockSpec((rows, d), lambda j: (0, 0)),
                  pl.BlockSpec((d, tn), lambda j: (0, j)),
                  pl.BlockSpec((1, tn), lambda j: (0, j))],
        out_specs=pl.BlockSpec((rows, tn), lambda j: (0, j)),
        compiler_params=_cparams(("arbitrary",)),
        name="adaln_mod",
    )(cc, w_ada_bf, b_ada)


def _mod_spec(which, tm, n_ctx_tok, lat_len):
    nct = n_ctx_tok // tm
    per = lat_len // tm

    def imap(i, *_):
        row = jnp.where(i < nct, 0, 1 + (i - nct) // per)
        return (row, which, 0, 0)

    return pl.BlockSpec((None, None, 1, D_MODEL), imap)


def _inproj_body(x_ref, sh_ref, sc_ref, g_ref, w_ref, o_ref, h_ref):
    @pl.when(pl.program_id(1) == 0)
    def _():
        h = _rms(x_ref[...]) * g_ref[...] * (1.0 + sc_ref[...]) + sh_ref[...]
        h_ref[...] = h.astype(BF16)

    o_ref[...] = _dot(h_ref[...], w_ref[...])


def _stage_inproj(x_all, mod4, norm1_g, w_cat, n_ctx_tok, lat_len):
    t = x_all.shape[0]
    tm, tn = 512, 2048
    return pl.pallas_call(
        _inproj_body,
        out_shape=jax.ShapeDtypeStruct((t, D_IN_PAD), F32),
        grid=(t // tm, D_IN_PAD // tn),
        in_specs=[pl.BlockSpec((tm, D_MODEL), lambda i, j: (i, 0)),
                  _mod_spec(0, tm, n_ctx_tok, lat_len),
                  _mod_spec(1, tm, n_ctx_tok, lat_len),
                  pl.BlockSpec((1, D_MODEL), lambda i, j: (0, 0)),
                  pl.BlockSpec((D_MODEL, tn), lambda i, j: (0, j))],
        out_specs=pl.BlockSpec((tm, tn), lambda i, j: (i, j)),
        scratch_shapes=[pltpu.VMEM((tm, D_MODEL), BF16)],
        compiler_params=_cparams(("parallel", "arbitrary")),
        name="norm1_inproj",
    )(x_all, mod4, mod4, norm1_g, w_cat)


CONV_TILE = 256


def _conv_body(n_ctx_tiles, tiles_per_lat, ub_ref, uc_ref, ux_ref, pc_ref, px_ref, nc_ref, nx_ref,
               cw_ref, wo_ref, y_ref):
    i = pl.program_id(0)
    is_lat = i >= n_ctx_tiles
    jl = (i - n_ctx_tiles) % tiles_per_lat
    half = D_A // 2
    z = uc_ref[...] * ux_ref[...]
    cw = cw_ref[...]
    row = lax.broadcasted_iota(jnp.int32, (CONV_TILE, 1), 0)
    pmask = jnp.where(is_lat, GRID_W - 1, CONV_TILE - 1)
    keep_p = (row & pmask) != 0
    keep_n = (row & pmask) != pmask
    zp = jnp.where(keep_p, pltpu.roll(z, 1, 0), 0.0)
    zn = jnp.where(keep_n, pltpu.roll(z, CONV_TILE - 1, 0), 0.0)
    conv_h = cw[0:1] * zp + cw[1:2] * z + cw[2:3] * zn
    zv = z[:, half:]
    hp = jnp.where(jl != 0, pc_ref[...] * px_ref[...], 0.0)
    hn = jnp.where(jl != tiles_per_lat - 1, nc_ref[...] * nx_ref[...], 0.0)
    vp = jnp.concatenate([hp, zv[:CONV_TILE - GRID_W]], axis=0)
    vn = jnp.concatenate([zv[GRID_W:], hn], axis=0)
    conv_v = cw[0:1, half:] * vp + cw[1:2, half:] * zv + cw[2:3, half:] * vn
    zc = jnp.concatenate([conv_h[:, :half], jnp.where(is_lat, conv_v, conv_h[:, half:])], axis=1)
    y_ref[...] = _dot((ub_ref[...] * zc).astype(BF16), wo_ref[...])


def _stage_conv(u, conv_w, w_out_a_bf, n_ctx_tok, lat_len):
    t = u.shape[0]
    tm = CONV_TILE
    n_ctx_tiles = n_ctx_tok // tm
    tiles_per_lat = lat_len // tm
    rpt = tm // GRID_W
    last_halo = t // GRID_W - 1
    half = D_A // 2
    cb = lambda col: col // D_A
    hb = lambda col: (col + half) // half
    body = functools.partial(_conv_body, n_ctx_tiles, tiles_per_lat)
    prev_map = lambda col: (lambda i: (jnp.maximum(i * rpt - 1, 0), hb(col)))
    next_map = lambda col: (lambda i: (jnp.minimum(i * rpt + rpt, last_halo), hb(col)))
    return pl.pallas_call(
        body,
        out_shape=jax.ShapeDtypeStruct((t, D_MODEL), F32),
        grid=(t // tm,),
        in_specs=[pl.BlockSpec((tm, D_A), lambda i: (i, cb(COL_UB))),
                  pl.BlockSpec((tm, D_A), lambda i: (i, cb(COL_UC))),
                  pl.BlockSpec((tm, D_A), lambda i: (i, cb(COL_UX))),
                  pl.BlockSpec((GRID_W, half), prev_map(COL_UC)),
                  pl.BlockSpec((GRID_W, half), prev_map(COL_UX)),
                  pl.BlockSpec((GRID_W, half), next_map(COL_UC)),
                  pl.BlockSpec((GRID_W, half), next_map(COL_UX)),
                  pl.BlockSpec((3, D_A), lambda i: (0, 0)),
                  pl.BlockSpec((D_A, D_MODEL), lambda i: (0, 0))],
        out_specs=pl.BlockSpec((tm, D_MODEL), lambda i: (i, 0)),
        compiler_params=_cparams(("parallel",)),
        name="short_conv",
    )(u, u, u, u, u, u, u, conv_w, w_out_a_bf)


def _rwkv_pre_body(lora_ref, w0_ref, wup_ref, a0_ref, aup_ref, lwf_ref, lwb_ref, alf_ref, alb_ref):
    lo = lora_ref[...]
    lw_refs = (lwf_ref, lwb_ref)
    al_refs = (alf_ref, alb_ref)
    for d in range(2):
        wd = lo[:, d * LORA_PAD:(d + 1) * LORA_PAD]
        ad = lo[:, (2 + d) * LORA_PAD:(3 + d) * LORA_PAD]
        xw = w0_ref[d:d + 1, :] + _dot(jnp.tanh(wd).astype(BF16), wup_ref[d])
        lw_refs[d][...] = -math.exp(-0.5) * jax.nn.sigmoid(xw)
        al_refs[d][...] = jax.nn.sigmoid(a0_ref[d:d + 1, :] + _dot(ad.astype(BF16), aup_ref[d]))


def _stage_rwkv_pre(u, w0, w_up_bf, a0, a_up_bf):
    t = u.shape[0]
    tm = 512
    out = jax.ShapeDtypeStruct((t, D_B), F32)
    ospec = pl.BlockSpec((tm, D_B), lambda i: (i, 0))
    return pl.pallas_call(
        _rwkv_pre_body,
        out_shape=(out, out, out, out),
        grid=(t // tm,),
        in_specs=[pl.BlockSpec((tm, LORA_BLOCK), lambda i: (i, COL_LORA // LORA_BLOCK)),
                  pl.BlockSpec((2, D_B), lambda i: (0, 0)),
                  pl.BlockSpec((2, LORA_PAD, D_B), lambda i: (0, 0, 0)),
                  pl.BlockSpec((2, D_B), lambda i: (0, 0)),
                  pl.BlockSpec((2, LORA_PAD, D_B), lambda i: (0, 0, 0))],
        out_specs=(ospec, ospec, ospec, ospec),
        compiler_params=_cparams(("parallel",)),
        name="rwkv_pre",
    )(u, w0, w_up_bf, a0, a_up_bf)


def _scan_phase1(reverse, units, kk_p, ka_p, rk_p, c):
    ones_bf, bdmask, tri_bf, strict4, incl4, eye4 = c
    n = len(units)
    rng = range(n)

    def bd(x_bf):
        return jnp.concatenate([x_bf] * HEADS_PER_GROUP, axis=0) * ones_bf

    def bd2(x, y):
        return jnp.concatenate([bd(x.astype(BF16)), bd(y.astype(BF16))], axis=1)

    r = [u[0] for u in units]
    k = [u[1] for u in units]
    v = [u[2] for u in units]
    lw = [u[3] for u in units]
    al = [u[4] for u in units]
    kraw = [k[i] * kk_p for i in rng]
    ss = [_headsum(kraw[i] * kraw[i], ones_bf) for i in rng]
    kkn = [kraw[i] * lax.rsqrt(ss[i] + 1e-12) for i in rng]
    kd = [k[i] * (1.0 + (al[i] - 1.0) * ka_p) for i in rng]
    b = [kkn[i] * al[i] for i in rng]
    bonus = [_headsum(r[i] * kd[i] * rk_p, ones_bf) * v[i] for i in rng]

    parts = [_split(lw[i], 3) for i in rng]
    lc = [sum(_dot(tri_bf, p) for p in parts[i]) for i in rng]
    ltot = [(lc[i][0:1] if reverse else lc[i][CHUNK - 1:CHUNK]) for i in rng]
    decay_col = [jnp.broadcast_to(jnp.exp(ltot[i]), (GROUP, GROUP)).T for i in rng]
    rt = [r[i] * jnp.exp(lc[i]) for i in rng]
    at = [-kkn[i] * jnp.exp(lc[i] - lw[i]) for i in rng]
    einv = [jnp.exp(-lc[i]) for i in rng]
    bh = [b[i] * einv[i] for i in rng]
    kh = [kd[i] * einv[i] for i in rng]
    etail = [jnp.exp(ltot[i] - lc[i]) for i in rng]
    bt = [(b[i] * etail[i]).astype(BF16) for i in rng]
    kt = [(kd[i] * etail[i]).astype(BF16) for i in rng]
    v_bf = [v[i].astype(BF16) for i in rng]

    lhs1 = [jnp.concatenate([at[i], rt[i]], axis=0).astype(BF16) for i in rng]
    ab = [_dot_nt(lhs1[i], bd(bh[i].astype(BF16))) for i in rng]
    ak = [_dot_nt(lhs1[i], bd(kh[i].astype(BF16))) for i in rng]
    a_ab = [jnp.where(strict4, ab[i][:CHUNK], 0.0) for i in rng]
    a_rb = [jnp.where(incl4, ab[i][CHUNK:], 0.0).astype(BF16) for i in rng]
    a_ak = [jnp.where(strict4, ak[i][:CHUNK], 0.0).astype(BF16) for i in rng]
    a_rk = [jnp.where(incl4, ak[i][CHUNK:], 0.0).astype(BF16) for i in rng]

    x = a_ab
    tinv = [eye4 + x[i] for i in rng]
    x = [_dot(x[i].astype(BF16), bd(x[i].astype(BF16))) for i in rng]
    for _ in range(4):
        y = [_dot(jnp.concatenate([x[i], tinv[i]], axis=0).astype(BF16), bd(x[i].astype(BF16))) for i in rng]
        x = [y[i][:CHUNK] for i in rng]
        tinv = [tinv[i] + y[i][CHUNK:] for i in rng]
    tinv = [(tinv[i] + _dot(tinv[i].astype(BF16), bd(x[i].astype(BF16)))).astype(BF16) for i in rng]

    bdv = [bd(v_bf[i]) for i in rng]
    akv = [_dot(a_ak[i], bdv[i]) for i in rng]
    wu = [_dot(tinv[i], bd2(at[i], akv[i])) for i in rng]
    wu_bf = [wu[i].astype(BF16) for i in rng]
    qo = [_dot(a_rb[i], bd2(wu[i][:, :GROUP], wu[i][:, GROUP:])) for i in rng]
    q = [(rt[i] + qo[i][:, :GROUP]).astype(BF16) for i in rng]
    o0 = [qo[i][:, GROUP:] + _dot(a_rk[i], bdv[i]) for i in rng]
    mg = [_dot_tn(bt[i], wu_bf[i]) for i in rng]
    kv = [_dot_tn(kt[i], v_bf[i]) for i in rng]
    m_bd = [jnp.where(bdmask, mg[i][:, :GROUP], 0.0).astype(BF16) for i in rng]
    g_bd = [jnp.where(bdmask, mg[i][:, GROUP:] + kv[i], 0.0) for i in rng]
    return [(q[i], o0[i], m_bd[i], g_bd[i], decay_col[i], bonus[i]) for i in rng]


def _scan_body(reverse, has_prev, n_ctx_blk, blk_per_ctx, blk_per_lat, n_blk, *refs):
    if has_prev:
        (r_ref, k_ref, v_ref, lw_ref, al_ref, s0_ref, kk_ref, ka_ref, rk_ref, op_ref, bp_ref,
         o_ref, bon_ref, sfin_ref, h_ref) = refs
    else:
        (r_ref, k_ref, v_ref, lw_ref, al_ref, s0_ref, kk_ref, ka_ref, rk_ref,
         o_ref, bon_ref, sfin_ref, h_ref) = refs
        op_ref = bp_ref = None
    s = pl.program_id(1)
    j = (n_blk - 1 - s) if reverse else s
    is_ctx = j < n_ctx_blk
    first_c = (blk_per_ctx - 1) if reverse else 0
    first_l = (blk_per_lat - 1) if reverse else 0
    start = jnp.where(is_ctx, (j % blk_per_ctx) == first_c, ((j - n_ctx_blk) % blk_per_lat) == first_l)

    @pl.when(start)
    def _():
        h_ref[...] = jnp.where(is_ctx, 0.0, s0_ref[...])

    ones_mask = _head_ones()
    ones_bf = ones_mask.astype(F32).astype(BF16)
    t_i = lax.broadcasted_iota(jnp.int32, (CHUNK, CHUNK), 0)
    s_i = lax.broadcasted_iota(jnp.int32, (CHUNK, CHUNK), 1)
    tri = (s_i >= t_i) if reverse else (s_i <= t_i)
    tri_bf = tri.astype(F32).astype(BF16)
    t4 = lax.broadcasted_iota(jnp.int32, (CHUNK, GROUP), 0)
    s4 = lax.broadcasted_iota(jnp.int32, (CHUNK, GROUP), 1) % HEAD_DIM
    incl4 = (s4 >= t4) if reverse else (s4 <= t4)
    strict4 = (s4 > t4) if reverse else (s4 < t4)
    eye4 = (s4 == t4).astype(F32)
    consts = (ones_bf, ones_mask, tri_bf, strict4, incl4, eye4)

    n_chunks = SCAN_BLOCK // CHUNK
    order = list(range(n_chunks - 1, -1, -1) if reverse else range(n_chunks))
    sls = [slice(cidx * CHUNK, (cidx + 1) * CHUNK) for cidx in order]
    units = [(r_ref[sl, :], k_ref[sl, :], v_ref[sl, :], lw_ref[sl, :], al_ref[sl, :]) for sl in sls]
    pre = _scan_phase1(reverse, units, kk_ref[...], ka_ref[...], rk_ref[...], consts)

    h_state = h_ref[...]
    for sl, (q, o0, m_bd, g_bd, decay_col, bonus) in zip(sls, pre):
        h_bf = h_state.astype(BF16)
        o = _dot(q, h_bf) + o0
        h_state = decay_col * h_state + _dot(m_bd, h_bf) + g_bd
        if has_prev:
            o = o + op_ref[sl, :]
            bonus = bonus + bp_ref[sl, :]
        o_ref[sl, :] = o
        bon_ref[sl, :] = bonus
    h_ref[...] = h_state
    sfin_ref[...] = h_state


def _stage_scan(reverse, u, lw, al, s0_bd, k_k, k_a, r_k, prev, n_ctx_seq, ctx_len, n_lat_seq, lat_len):
    t = u.shape[0]
    blk_per_ctx = ctx_len // SCAN_BLOCK
    blk_per_lat = lat_len // SCAN_BLOCK
    n_ctx_blk = n_ctx_seq * blk_per_ctx
    n_blk = t // SCAN_BLOCK
    has_prev = prev is not None

    def jmap(s):
        return (n_blk - 1 - s) if reverse else s

    def tok(colbase):
        return pl.BlockSpec((SCAN_BLOCK, GROUP), lambda g, s: (jmap(s), colbase // GROUP + g))

    def s0_map(g, s):
        j = jmap(s)
        b = jnp.clip((j - n_ctx_blk) // blk_per_lat, 0, n_lat_seq - 1)
        return (b, g, 0, 0)

    def sfin_map(g, s):
        j = jmap(s)
        b = jnp.where(j < n_ctx_blk, j // blk_per_ctx, n_ctx_seq)
        return (b, g, 0, 0)

    par = pl.BlockSpec((1, GROUP), lambda g, s: (0, g))
    in_specs = [tok(COL_R), tok(COL_K), tok(COL_V), tok(0), tok(0),
                pl.BlockSpec((None, None, GROUP, GROUP), s0_map), par, par, par]
    args = [u, u, u, lw, al, s0_bd, k_k, k_a, r_k]
    if has_prev:
        in_specs += [tok(0), tok(0)]
        args += list(prev)
    out = jax.ShapeDtypeStruct((t, D_B), F32)
    body = functools.partial(_scan_body, reverse, has_prev, n_ctx_blk, blk_per_ctx, blk_per_lat, n_blk)
    return pl.pallas_call(
        body,
        out_shape=(out, out, jax.ShapeDtypeStruct((n_ctx_seq + 1, N_GROUPS, GROUP, GROUP), F32)),
        grid=(N_GROUPS, n_blk),
        in_specs=in_specs,
        out_specs=(tok(0), tok(0), pl.BlockSpec((None, None, GROUP, GROUP), sfin_map)),
        scratch_shapes=[pltpu.VMEM((GROUP, GROUP), F32)],
        compiler_params=_cparams(("parallel", "arbitrary")),
        name="rwkv_scan_bwd" if reverse else "rwkv_scan_fwd",
    )(*args)


def _post_body(o_ref, bon_ref, lora_ref, lng_ref, lnb_ref, gup_ref, wo_ref, y_ref):
    ones_bf = _head_ones().astype(F32).astype(BF16)
    o = o_ref[...]
    mu = _headsum(o, ones_bf) * (1.0 / HEAD_DIM)
    d = o - mu
    var = _headsum(d * d, ones_bf) * (1.0 / HEAD_DIM)
    on = d * lax.rsqrt(var + GN_EPS) * lng_ref[...] + lnb_ref[...] + bon_ref[...]
    gd = lora_ref[:, 4 * LORA_PAD:4 * LORA_PAD + LORA_G]
    g = _dot(jax.nn.sigmoid(gd).astype(BF16), gup_ref[...])
    y_ref[...] = _dot((on * g).astype(BF16), wo_ref[...])


def _stage_post(o, bon, u, ln_g, ln_b, g_up_bf, w_o_bf):
    t = o.shape[0]
    tm = 256
    return pl.pallas_call(
        _post_body,
        out_shape=jax.ShapeDtypeStruct((t, D_MODEL), F32),
        grid=(t // tm,),
        in_specs=[pl.BlockSpec((tm, D_B), lambda i: (i, 0)),
                  pl.BlockSpec((tm, D_B), lambda i: (i, 0)),
                  pl.BlockSpec((tm, LORA_BLOCK), lambda i: (i, COL_LORA // LORA_BLOCK)),
                  pl.BlockSpec((1, D_B), lambda i: (0, 0)),
                  pl.BlockSpec((1, D_B), lambda i: (0, 0)),
                  pl.BlockSpec((LORA_G, D_B), lambda i: (0, 0)),
                  pl.BlockSpec((D_B, D_MODEL), lambda i: (0, 0))],
        out_specs=pl.BlockSpec((tm, D_MODEL), lambda i: (i, 0)),
        compiler_params=_cparams(("parallel",)),
        name="rwkv_post",
    )(o, bon, u, ln_g, ln_b, g_up_bf, w_o_bf)


def _merge_body(ya_ref, yb_ref, ga_ref, gb_ref, x_ref, gt_ref, sh_ref, sc_ref, g_ref, w_ref, x1_ref, h2_ref):
    m = jax.nn.sigmoid(ga_ref[...]) * ya_ref[...] + jax.nn.sigmoid(gb_ref[...]) * yb_ref[...]
    x1 = x_ref[...] + gt_ref[...] * _dot(m.astype(BF16), w_ref[...])
    x1_ref[...] = x1
    h2_ref[...] = _rms(x1) * g_ref[...] * (1.0 + sc_ref[...]) + sh_ref[...]


def _stage_merge(y_a, y_b, u, x_all, mod4, norm2_g, w_mix_bf, n_ctx_tok, lat_len):
    t = x_all.shape[0]
    tm = 256
    full = lambda col: pl.BlockSpec((tm, D_MODEL), lambda i: (i, col // D_MODEL))
    out = jax.ShapeDtypeStruct((t, D_MODEL), F32)
    return pl.pallas_call(
        _merge_body,
        out_shape=(out, out),
        grid=(t // tm,),
        in_specs=[full(0), full(0), full(COL_GA), full(COL_GB), full(0),
                  _mod_spec(2, tm, n_ctx_tok, lat_len),
                  _mod_spec(3, tm, n_ctx_tok, lat_len),
                  _mod_spec(4, tm, n_ctx_tok, lat_len),
                  pl.BlockSpec((1, D_MODEL), lambda i: (0, 0)),
                  pl.BlockSpec((D_MODEL, D_MODEL), lambda i: (0, 0))],
        out_specs=(full(0), full(0)),
        compiler_params=_cparams(("parallel",)),
        name="merge_norm2",
    )(y_a, y_b, u, u, x_all, mod4, mod4, mod4, norm2_g, w_mix_bf)


TOPK_TOKENS = 256


def _extract_topk(s, pos, limit, payload=None):
    vals, outs = [], []
    for _ in range(PEER_TOPK):
        m = jnp.max(s, axis=0, keepdims=True)
        pm = jnp.min(jnp.where(s == m, pos, limit), axis=0, keepdims=True)
        sel = pos == pm
        vals.append(m)
        if payload is None:
            outs.append(pm)
        else:
            outs.append(jnp.max(jnp.where(sel, payload, -1), axis=0, keepdims=True))
        s = jnp.where(sel, -jnp.inf, s)
    return jnp.concatenate(vals, axis=0), jnp.concatenate(outs, axis=0)


def _topk_body(h_ref, wq_ref, keys_ref, eid_ref, gate_ref, q_ref):
    q_ref[...] = _dot_nt(wq_ref[...], h_ref[...].astype(BF16))
    kpos = lax.broadcasted_iota(jnp.int32, (N_KEYS, TOPK_TOKENS), 0)
    half = PEER_TOPK // 2
    iota = lambda rows: lax.broadcasted_iota(jnp.int32, (rows, TOPK_TOKENS), 0)
    cpos = jnp.concatenate([iota(PEER_TOPK)] + [iota(half) + PEER_TOPK * i for i in range(1, half)]
                           + [(iota(half) + half) * PEER_TOPK], axis=0)

    def head(hd, carry):
        tops = []
        for p in range(2):
            off = pl.multiple_of(hd * (2 * HALF_KEY) + p * HALF_KEY, HALF_KEY)
            qs = q_ref[pl.ds(off, HALF_KEY), :].astype(BF16)
            s = _dot(keys_ref[p * PEER_HEADS + hd], qs)
            tops.append(_extract_topk(s, kpos, N_KEYS))
        (s1, i1), (s2, i2) = tops
        cand = jnp.concatenate([s1[0:1] + s2] + [s1[i:i + 1] + s2[:half] for i in range(1, half)]
                               + [s1[half:] + s2[0:1]], axis=0)
        ids = jnp.concatenate([i1[0:1] * N_KEYS + i2] + [i1[i:i + 1] * N_KEYS + i2[:half] for i in range(1, half)]
                              + [i1[half:] * N_KEYS + i2[0:1]], axis=0)
        tv, te = _extract_topk(cand, cpos, PEER_TOPK * PEER_TOPK, payload=ids)
        ex = jnp.exp(tv - tv[0:1])
        gate = ex / jnp.sum(ex, axis=0, keepdims=True)
        row = pl.multiple_of(hd * PEER_TOPK, PEER_TOPK)
        eid_ref[pl.ds(row, PEER_TOPK), :] = te
        gate_ref[pl.ds(row, PEER_TOPK), :] = gate
        return carry

    lax.fori_loop(0, PEER_HEADS, head, 0)


def _stage_topk(h2, w_q_t_bf, keys_bf):
    t = h2.shape[0]
    tb = TOPK_TOKENS
    return pl.pallas_call(
        _topk_body,
        out_shape=(jax.ShapeDtypeStruct((EXPERTS_PER_TOKEN, t), jnp.int32),
                   jax.ShapeDtypeStruct((EXPERTS_PER_TOKEN, t), F32)),
        grid=(t // tb,),
        in_specs=[pl.BlockSpec((tb, D_MODEL), lambda i: (i, 0)),
                  pl.BlockSpec((PEER_HEADS * 2 * HALF_KEY, D_MODEL), lambda i: (0, 0)),
                  pl.BlockSpec((2 * PEER_HEADS, N_KEYS, HALF_KEY), lambda i: (0, 0, 0))],
        out_specs=(pl.BlockSpec((EXPERTS_PER_TOKEN, tb), lambda i: (0, i)),
                   pl.BlockSpec((EXPERTS_PER_TOKEN, tb), lambda i: (0, i))),
        scratch_shapes=[pltpu.VMEM((PEER_HEADS * 2 * HALF_KEY, tb), F32)],
        compiler_params=_cparams(("parallel",)),
        name="peer_topk",
    )(h2, w_q_t_bf, keys_bf)


def _gather_copy(tab_ref, buf, sem, row, expert):
    return pltpu.make_async_copy(tab_ref.at[expert], buf.at[pl.ds(row, 1), :], sem)


def _gather_issue(tab_ref, id_ref, row0, buf, sem):
    for t in range(GATHER_TOKENS):
        for e in range(EXPERTS_PER_TOKEN):
            _gather_copy(tab_ref, buf, sem, t * EXPERTS_PER_TOKEN + e, id_ref[row0 + t, e]).start(priority=e % 2)


def _gather_wait(tab_ref, buf, sem):
    for r in range(GATHER_TOKENS * EXPERTS_PER_TOKEN):
        _gather_copy(tab_ref, buf, sem, r, 0).wait()


def _mix_tokens(buf, gate, h, eye):
    ne = EXPERTS_PER_TOKEN
    n_tiles = D_MODEL // LANE
    outs = []
    for t in range(GATHER_TOKENS):
        rows = slice(t * ne, (t + 1) * ne)
        acc = jnp.zeros((ne, LANE), F32)
        for j in range(n_tiles):
            w = buf[rows, j * LANE:(j + 1) * LANE]
            u = lax.bitcast_convert_type(w << 16, F32)
            acc = acc + u * h[t:t + 1, j * LANE:(j + 1) * LANE]
        pre = jnp.sum(acc, axis=-1, keepdims=True)
        gcol = jnp.sum(jnp.where(eye, gate[t:t + 1, :], 0.0), axis=-1, keepdims=True)
        coef = gcol * jax.nn.gelu(pre, approximate=True)
        cb = jnp.broadcast_to(coef, (ne, LANE))
        tiles = []
        for j in range(n_tiles):
            w = buf[rows, j * LANE:(j + 1) * LANE]
            v = lax.bitcast_convert_type(w & jnp.uint32(0xFFFF0000), F32)
            tiles.append(jnp.sum(cb * v, axis=0, keepdims=True))
        outs.append(jnp.concatenate(tiles, axis=1))
    return jnp.concatenate(outs, axis=0)


def _gather_body(ids_ref, idn_ref, gate_ref, h_ref, x_ref, gt_ref, fg_ref, tab_ref, o_ref,
                 buf0, buf1, sem):
    i = pl.program_id(0)
    n = pl.num_programs(0)
    tg = GATHER_TOKENS
    eye = (lax.broadcasted_iota(jnp.int32, (EXPERTS_PER_TOKEN, EXPERTS_PER_TOKEN), 0)
           == lax.broadcasted_iota(jnp.int32, (EXPERTS_PER_TOKEN, EXPERTS_PER_TOKEN), 1))

    @pl.when(i == 0)
    def _():
        _gather_issue(tab_ref, ids_ref, 0, buf0, sem.at[0])

    def finish(mix, lo):
        x2 = x_ref[lo:lo + tg, :] + gt_ref[...] * mix
        o_ref[lo:lo + tg, :] = _rms(x2) * fg_ref[...]

    _gather_wait(tab_ref, buf0, sem.at[0])
    _gather_issue(tab_ref, ids_ref, tg, buf1, sem.at[1])
    finish(_mix_tokens(buf0, gate_ref[0:tg, :], h_ref[0:tg, :], eye), 0)
    _gather_wait(tab_ref, buf1, sem.at[1])
    _gather_issue(tab_ref, idn_ref, 0, buf0, sem.at[0])
    finish(_mix_tokens(buf1, gate_ref[tg:2 * tg, :], h_ref[tg:2 * tg, :], eye), tg)

    @pl.when(i == n - 1)
    def _():
        _gather_wait(tab_ref, buf0, sem.at[0])


def _stage_gather(eid, gate, h2, x1, mod4, final_g, tab, n_ctx_tok, lat_len):
    t = h2.shape[0]
    tg = GATHER_TOKENS
    ts = 2 * tg
    n = t // ts
    tokspec = lambda w: pl.BlockSpec((ts, w), lambda i: (i, 0))
    rows = tg * EXPERTS_PER_TOKEN
    return pl.pallas_call(
        _gather_body,
        out_shape=jax.ShapeDtypeStruct((t, D_MODEL), F32),
        grid=(n,),
        in_specs=[pl.BlockSpec((ts, EXPERTS_PER_TOKEN), lambda i: (i, 0), memory_space=pltpu.SMEM),
                  pl.BlockSpec((tg, EXPERTS_PER_TOKEN), lambda i: (jnp.minimum(2 * i + 2, 2 * n - 1), 0),
                               memory_space=pltpu.SMEM),
                  tokspec(EXPERTS_PER_TOKEN), tokspec(D_MODEL), tokspec(D_MODEL),
                  _mod_spec(5, ts, n_ctx_tok, lat_len),
                  pl.BlockSpec((1, D_MODEL), lambda i: (0, 0)),
                  pl.BlockSpec(memory_space=pl.ANY)],
        out_specs=tokspec(D_MODEL),
        scratch_shapes=[pltpu.VMEM((rows, D_MODEL), jnp.uint32),
                        pltpu.VMEM((rows, D_MODEL), jnp.uint32),
                        pltpu.SemaphoreType.DMA((2,))],
        compiler_params=_cparams(("arbitrary",)),
        name="peer_gather",
    )(eid, eid, gate, h2, x1, mod4, final_g, tab)


def _pack_expert_tables(u_tab, v_tab):
    lo = lax.bitcast_convert_type(u_tab.astype(BF16), jnp.uint16).astype(jnp.uint32)
    hi = lax.bitcast_convert_type(v_tab.astype(BF16), jnp.uint16).astype(jnp.uint32)
    return (lo | (hi << 16))[:, None, :]


def _pad_rows(w, rows):
    return jnp.pad(w, ((0, 0), (0, rows - w.shape[1]), (0, 0)))


def _layer_weights(w_in, w_up, a_up):
    widths = (D_A, D_A, D_A, D_B, D_B, D_B, LORA_W, LORA_W, LORA_A, LORA_A, LORA_G, D_MODEL, D_MODEL)
    cuts = [0]
    for w in widths:
        cuts.append(cuts[-1] + w)
    seg = [w_in[:, cuts[i]:cuts[i + 1]] for i in range(len(widths))]
    (u_b, u_c, u_x, r, k, v, wd_f, wd_b, ad_f, ad_b, gd, gate_a, gate_b) = seg
    padc = lambda m: jnp.pad(m, ((0, 0), (0, LORA_PAD - m.shape[1])))
    tail = jnp.zeros((w_in.shape[0], LORA_BLOCK - 4 * LORA_PAD - LORA_G), w_in.dtype)
    w_cat = jnp.concatenate([r, k, v, gate_a, gate_b, u_b, u_c, u_x,
                             padc(wd_f), padc(wd_b), padc(ad_f), padc(ad_b), gd, tail], axis=1)
    return w_cat.astype(BF16), _pad_rows(w_up, LORA_PAD).astype(BF16), _pad_rows(a_up, LORA_PAD).astype(BF16)


def _state_to_blockdiag(s0):
    b = s0.shape[0]
    st = jnp.swapaxes(s0, -1, -2).reshape(b, N_GROUPS, HEADS_PER_GROUP, HEAD_DIM, HEAD_DIM)
    eye = jnp.eye(HEADS_PER_GROUP, dtype=s0.dtype)
    bd = jnp.einsum('bgikv,ij->bgikjv', st, eye)
    return bd.reshape(b, N_GROUPS, GROUP, GROUP)


def _blockdiag_to_state(bd):
    b = bd.shape[0]
    x = bd.reshape(b, N_GROUPS, HEADS_PER_GROUP, HEAD_DIM, HEADS_PER_GROUP, HEAD_DIM)
    d = jnp.diagonal(x, axis1=2, axis2=4)
    d = jnp.moveaxis(d, -1, 2)
    return jnp.swapaxes(d, -1, -2).reshape(b, H_B, HEAD_DIM, HEAD_DIM)


def kernel(x_prompt, x_sample, state_rwkv, c, c_ctx, w_ada, b_ada, norm1_g, norm2_g, w_in, conv_w, w_out_a, w0, w_up, a0, a_up, g_up, k_k, k_a, r_k, ln_g, ln_b, w_o, w_mix, w_q, sub_keys, u_tab, v_tab, final_g):
    n_ctx_seq, ctx_len, _ = x_prompt.shape
    n_lat_seq, lat_len, _ = x_sample.shape
    depth = w_in.shape[0]
    n_ctx_tok = n_ctx_seq * ctx_len
    assert depth == 1, "the final norm is fused into the last stage of a single layer"
    assert ctx_len % SCAN_BLOCK == 0 and lat_len % 512 == 0 and n_ctx_tok % 512 == 0
    assert lat_len % (GRID_W * (CONV_TILE // GRID_W)) == 0

    x_all = jnp.concatenate([x_prompt.reshape(n_ctx_tok, D_MODEL),
                             x_sample.reshape(n_lat_seq * lat_len, D_MODEL)], axis=0)
    mod_rows = 16
    cc = jnp.zeros((mod_rows, D_MODEL), F32).at[0].set(c_ctx).at[1:1 + n_lat_seq].set(c)
    new_states = []
    for l in range(depth):
        w_cat, w_up_bf, a_up_bf = _layer_weights(w_in[l], w_up[l], a_up[l])
        mod = _stage_mod(cc, w_ada[l].astype(BF16), b_ada[l][None, :])
        mod4 = mod.reshape(mod_rows, 6, 1, D_MODEL)
        u = _stage_inproj(x_all, mod4, norm1_g[l][None, :], w_cat, n_ctx_tok, lat_len)
        y_a = _stage_conv(u, conv_w[l], w_out_a[l].astype(BF16), n_ctx_tok, lat_len)
        lw_f, lw_b, al_f, al_b = _stage_rwkv_pre(u, w0[l], w_up_bf, a0[l], a_up_bf)
        kk2, ka2, rk2 = k_k[l][None, :], k_a[l][None, :], r_k[l].reshape(1, D_B)
        seq = (n_ctx_seq, ctx_len, n_lat_seq, lat_len)
        o_f, bon_f, sfin_f = _stage_scan(False, u, lw_f, al_f, _state_to_blockdiag(state_rwkv[:, l, 0]),
                                         kk2, ka2, rk2, None, *seq)
        o, bon, sfin_b = _stage_scan(True, u, lw_b, al_b, _state_to_blockdiag(state_rwkv[:, l, 1]),
                                     kk2, ka2, rk2, (o_f, bon_f), *seq)
        new_states.append(jnp.stack([_blockdiag_to_state(sfin_f[:n_ctx_seq]),
                                     _blockdiag_to_state(sfin_b[:n_ctx_seq])], axis=1))
        y_b = _stage_post(o, bon, u, ln_g[l][None, :], ln_b[l][None, :], g_up[l].astype(BF16), w_o[l].astype(BF16))
        x1, h2 = _stage_merge(y_a, y_b, u, x_all, mod4, norm2_g[l][None, :], w_mix[l].astype(BF16),
                              n_ctx_tok, lat_len)
        keys = sub_keys[l].reshape(2 * PEER_HEADS, N_KEYS, HALF_KEY).astype(BF16)
        eid_t, gate_t = _stage_topk(h2, w_q[l].T.astype(BF16), keys)
        tab = _pack_expert_tables(u_tab[l], v_tab[l])
        x_all = _stage_gather(eid_t.T, gate_t.T, h2, x1, mod4, final_g[None, :], tab, n_ctx_tok, lat_len)
    y_prompt = x_all[:n_ctx_tok].reshape(x_prompt.shape)
    y_sample = x_all[n_ctx_tok:].reshape(x_sample.shape)
    return (y_prompt, y_sample, jnp.stack(new_states, axis=1))
```

```python
import functools
import math

import jax
import jax.numpy as jnp
from jax import lax
from jax.experimental import pallas as pl
from jax.experimental.pallas import tpu as pltpu

F32 = jnp.float32
BF16 = jnp.bfloat16

D_MODEL = 2048
GRID_W = 64
D_A = 1024
H_B = 32
HEAD_DIM = 64
D_B = H_B * HEAD_DIM
LORA_W = 96
LORA_A = 96
LORA_G = 256
PEER_HEADS = 8
N_KEYS = 128
HALF_KEY = 128
PEER_TOPK = 16
RMS_EPS = 1e-6
GN_EPS = 64e-5

LANE = 128
CHUNK = 64
SCAN_BLOCK = 256
GROUP = 256
HEADS_PER_GROUP = GROUP // HEAD_DIM
N_GROUPS = D_B // GROUP
SCAN_GROUPS = 2
LORA_PAD = 128
EXPERTS_PER_TOKEN = PEER_HEADS * PEER_TOPK
GATHER_TOKENS = 8
VMEM_LIMIT = 56 * 1024 * 1024

COL_R, COL_K, COL_V, COL_GA, COL_GB = 0, 2048, 4096, 6144, 8192
COL_UB, COL_UC, COL_UX, COL_LORA = 10240, 11264, 12288, 13312
D_IN_PAD = 14336
LORA_BLOCK = 1024


def _cparams(sem, vmem=VMEM_LIMIT):
    return pltpu.CompilerParams(dimension_semantics=sem, vmem_limit_bytes=vmem)


def _dot(a, b):
    return jnp.dot(a, b, preferred_element_type=F32)


def _dot_nt(a, b):
    return lax.dot_general(a, b, (((1,), (1,)), ((), ())), preferred_element_type=F32)


def _dot_tn(a, b):
    return lax.dot_general(a, b, (((0,), (0,)), ((), ())), preferred_element_type=F32)


def _split(x, parts):
    out = []
    for _ in range(parts):
        p = x.astype(BF16)
        out.append(p)
        x = x - p.astype(F32)
    return out


def _head_ones():
    r = lax.broadcasted_iota(jnp.int32, (GROUP, GROUP), 0) // HEAD_DIM
    c = lax.broadcasted_iota(jnp.int32, (GROUP, GROUP), 1) // HEAD_DIM
    return r == c


def _headsum(x, ones_bf):
    outs = []
    for g in range(x.shape[1] // GROUP):
        xs = x[:, g * GROUP:(g + 1) * GROUP]
        hi, lo = _split(xs, 2)
        outs.append(_dot(hi, ones_bf) + _dot(lo, ones_bf))
    return outs[0] if len(outs) == 1 else jnp.concatenate(outs, axis=1)


def _rms(x):
    return x * lax.rsqrt(jnp.mean(x * x, axis=-1, keepdims=True) + RMS_EPS)


def _mod_body(c_ref, w_ref, b_ref, o_ref):
    c = c_ref[...]
    s = (c * jax.nn.sigmoid(c)).astype(BF16)
    o_ref[...] = _dot(s, w_ref[...]) + b_ref[...]


def _stage_mod(cc, w_ada_bf, b_ada):
    rows, d = cc.shape
    n = w_ada_bf.shape[1]
    tn = 2048
    return pl.pallas_call(
        _mod_body,
        out_shape=jax.ShapeDtypeStruct((rows, n), F32),
        grid=(n // tn,),
        in_specs=[pl.BlockSpec((rows, d), lambda j: (0, 0)),
                  pl.BlockSpec((d, tn), lambda j: (0, j)),
                  pl.BlockSpec((1, tn), lambda j: (0, j))],
        out_specs=pl.BlockSpec((rows, tn), lambda j: (0, j)),
        compiler_params=_cparams(("arbitrary",)),
        name="adaln_mod",
    )(cc, w_ada_bf, b_ada)


def _mod_spec(which, tm, n_ctx_tok, lat_len):
    nct = n_ctx_tok // tm
    per = lat_len // tm

    def imap(i, *_):
        row = jnp.where(i < nct, 0, 1 + (i - nct) // per)
        return (row, which, 0, 0)

    return pl.BlockSpec((None, None, 1, D_MODEL), imap)


def _inproj_body(x_ref, sh_ref, sc_ref, g_ref, w_ref, o_ref, h_ref):
    @pl.when(pl.program_id(1) == 0)
    def _():
        h = _rms(x_ref[...]) * g_ref[...] * (1.0 + sc_ref[...]) + sh_ref[...]
        h_ref[...] = h.astype(BF16)

    o_ref[...] = _dot(h_ref[...], w_ref[...])


def _stage_inproj(x_all, mod4, norm1_g, w_cat, n_ctx_tok, lat_len):
    t = x_all.shape[0]
    tm, tn = 512, 2048
    return pl.pallas_call(
        _inproj_body,
        out_shape=jax.ShapeDtypeStruct((t, D_IN_PAD), F32),
        grid=(t // tm, D_IN_PAD // tn),
        in_specs=[pl.BlockSpec((tm, D_MODEL), lambda i, j: (i, 0)),
                  _mod_spec(0, tm, n_ctx_tok, lat_len),
                  _mod_spec(1, tm, n_ctx_tok, lat_len),
                  pl.BlockSpec((1, D_MODEL), lambda i, j: (0, 0)),
                  pl.BlockSpec((D_MODEL, tn), lambda i, j: (0, j))],
        out_specs=pl.BlockSpec((tm, tn), lambda i, j: (i, j)),
        scratch_shapes=[pltpu.VMEM((tm, D_MODEL), BF16)],
        compiler_params=_cparams(("parallel", "arbitrary")),
        name="norm1_inproj",
    )(x_all, mod4, mod4, norm1_g, w_cat)


CONV_TILE = 256


def _conv_body(n_ctx_tiles, tiles_per_lat, ub_ref, uc_ref, ux_ref, pc_ref, px_ref, nc_ref, nx_ref,
               cw_ref, wo_ref, y_ref):
    i = pl.program_id(0)
    is_lat = i >= n_ctx_tiles
    jl = (i - n_ctx_tiles) % tiles_per_lat
    half = D_A // 2
    z = uc_ref[...] * ux_ref[...]
    cw = cw_ref[...]
    row = lax.broadcasted_iota(jnp.int32, (CONV_TILE, 1), 0)
    pmask = jnp.where(is_lat, GRID_W - 1, CONV_TILE - 1)
    keep_p = (row & pmask) != 0
    keep_n = (row & pmask) != pmask
    zp = jnp.where(keep_p, pltpu.roll(z, 1, 0), 0.0)
    zn = jnp.where(keep_n, pltpu.roll(z, CONV_TILE - 1, 0), 0.0)
    conv_h = cw[0:1] * zp + cw[1:2] * z + cw[2:3] * zn
    zv = z[:, half:]
    hp = jnp.where(jl != 0, pc_ref[...] * px_ref[...], 0.0)
    hn = jnp.where(jl != tiles_per_lat - 1, nc_ref[...] * nx_ref[...], 0.0)
    vp = jnp.concatenate([hp, zv[:CONV_TILE - GRID_W]], axis=0)
    vn = jnp.concatenate([zv[GRID_W:], hn], axis=0)
    conv_v = cw[0:1, half:] * vp + cw[1:2, half:] * zv + cw[2:3, half:] * vn
    zc = jnp.concatenate([conv_h[:, :half], jnp.where(is_lat, conv_v, conv_h[:, half:])], axis=1)
    y_ref[...] = _dot((ub_ref[...] * zc).astype(BF16), wo_ref[...])


def _stage_conv(u, conv_w, w_out_a_bf, n_ctx_tok, lat_len):
    t = u.shape[0]
    tm = CONV_TILE
    n_ctx_tiles = n_ctx_tok // tm
    tiles_per_lat = lat_len // tm
    rpt = tm // GRID_W
    last_halo = t // GRID_W - 1
    half = D_A // 2
    cb = lambda col: col // D_A
    hb = lambda col: (col + half) // half
    body = functools.partial(_conv_body, n_ctx_tiles, tiles_per_lat)
    prev_map = lambda col: (lambda i: (jnp.maximum(i * rpt - 1, 0), hb(col)))
    next_map = lambda col: (lambda i: (jnp.minimum(i * rpt + rpt, last_halo), hb(col)))
    return pl.pallas_call(
        body,
        out_shape=jax.ShapeDtypeStruct((t, D_MODEL), F32),
        grid=(t // tm,),
        in_specs=[pl.BlockSpec((tm, D_A), lambda i: (i, cb(COL_UB))),
                  pl.BlockSpec((tm, D_A), lambda i: (i, cb(COL_UC))),
                  pl.BlockSpec((tm, D_A), lambda i: (i, cb(COL_UX))),
                  pl.BlockSpec((GRID_W, half), prev_map(COL_UC)),
                  pl.BlockSpec((GRID_W, half), prev_map(COL_UX)),
                  pl.BlockSpec((GRID_W, half), next_map(COL_UC)),
                  pl.BlockSpec((GRID_W, half), next_map(COL_UX)),
                  pl.BlockSpec((3, D_A), lambda i: (0, 0)),
                  pl.BlockSpec((D_A, D_MODEL), lambda i: (0, 0))],
        out_specs=pl.BlockSpec((tm, D_MODEL), lambda i: (i, 0)),
        compiler_params=_cparams(("parallel",)),
        name="short_conv",
    )(u, u, u, u, u, u, u, conv_w, w_out_a_bf)


def _rwkv_pre_body(lora_ref, w0_ref, wup_ref, a0_ref, aup_ref, lwf_ref, lwb_ref, alf_ref, alb_ref):
    lo = lora_ref[...]
    lw_refs = (lwf_ref, lwb_ref)
    al_refs = (alf_ref, alb_ref)
    for d in range(2):
        wd = lo[:, d * LORA_PAD:(d + 1) * LORA_PAD]
        ad = lo[:, (2 + d) * LORA_PAD:(3 + d) * LORA_PAD]
        xw = w0_ref[d:d + 1, :] + _dot(jnp.tanh(wd).astype(BF16), wup_ref[d])
        lw_refs[d][...] = -math.exp(-0.5) * jax.nn.sigmoid(xw)
        al_refs[d][...] = jax.nn.sigmoid(a0_ref[d:d + 1, :] + _dot(ad.astype(BF16), aup_ref[d]))


def _stage_rwkv_pre(u, w0, w_up_bf, a0, a_up_bf):
    t = u.shape[0]
    tm = 512
    out = jax.ShapeDtypeStruct((t, D_B), F32)
    ospec = pl.BlockSpec((tm, D_B), lambda i: (i, 0))
    return pl.pallas_call(
        _rwkv_pre_body,
        out_shape=(out, out, out, out),
        grid=(t // tm,),
        in_specs=[pl.BlockSpec((tm, LORA_BLOCK), lambda i: (i, COL_LORA // LORA_BLOCK)),
                  pl.BlockSpec((2, D_B), lambda i: (0, 0)),
                  pl.BlockSpec((2, LORA_PAD, D_B), lambda i: (0, 0, 0)),
                  pl.BlockSpec((2, D_B), lambda i: (0, 0)),
                  pl.BlockSpec((2, LORA_PAD, D_B), lambda i: (0, 0, 0))],
        out_specs=(ospec, ospec, ospec, ospec),
        compiler_params=_cparams(("parallel",)),
        name="rwkv_pre",
    )(u, w0, w_up_bf, a0, a_up_bf)


def _scan_phase1(reverse, units, c):
    ones_bf, bdmask, tri_bf, strict4, incl4, eye4 = c
    n = len(units)
    rng = range(n)
    kk_p = [u[5] for u in units]
    ka_p = [u[6] for u in units]
    rk_p = [u[7] for u in units]

    def bd(x_bf):
        return jnp.concatenate([x_bf] * HEADS_PER_GROUP, axis=0) * ones_bf

    def bd2(x, y):
        return jnp.concatenate([bd(x.astype(BF16)), bd(y.astype(BF16))], axis=1)

    r = [u[0] for u in units]
    k = [u[1] for u in units]
    v = [u[2] for u in units]
    lw = [u[3] for u in units]
    al = [u[4] for u in units]
    kraw = [k[i] * kk_p[i] for i in rng]
    kd = [k[i] * (1.0 + (al[i] - 1.0) * ka_p[i]) for i in rng]
    hs = [_dot(jnp.concatenate(_split(kraw[i] * kraw[i], 2) + _split(r[i] * kd[i] * rk_p[i], 2), axis=0), ones_bf)
          for i in rng]
    kkn = [kraw[i] * lax.rsqrt(hs[i][:CHUNK] + hs[i][CHUNK:2 * CHUNK] + 1e-12) for i in rng]
    b = [kkn[i] * al[i] for i in rng]
    bonus = [(hs[i][2 * CHUNK:3 * CHUNK] + hs[i][3 * CHUNK:]) * v[i] for i in rng]

    parts = [_split(lw[i], 3) for i in rng]
    lc = [sum(_dot(tri_bf, p) for p in parts[i]) for i in rng]
    ltot = [(lc[i][0:1] if reverse else lc[i][CHUNK - 1:CHUNK]) for i in rng]
    decay_col = [jnp.broadcast_to(jnp.exp(ltot[i]), (GROUP, GROUP)).T for i in rng]
    rt = [r[i] * jnp.exp(lc[i]) for i in rng]
    at = [-kkn[i] * jnp.exp(lc[i] - lw[i]) for i in rng]
    einv = [jnp.exp(-lc[i]) for i in rng]
    bh = [b[i] * einv[i] for i in rng]
    kh = [kd[i] * einv[i] for i in rng]
    etail = [jnp.exp(ltot[i] - lc[i]) for i in rng]
    bt = [(b[i] * etail[i]).astype(BF16) for i in rng]
    kt = [(kd[i] * etail[i]).astype(BF16) for i in rng]
    v_bf = [v[i].astype(BF16) for i in rng]

    lhs1 = [jnp.concatenate([at[i], rt[i]], axis=0).astype(BF16) for i in rng]
    ab = [_dot_nt(lhs1[i], bd(bh[i].astype(BF16))) for i in rng]
    ak = [_dot_nt(lhs1[i], bd(kh[i].astype(BF16))) for i in rng]
    a_ab = [jnp.where(strict4, ab[i][:CHUNK], 0.0) for i in rng]
    a_rb = [jnp.where(incl4, ab[i][CHUNK:], 0.0).astype(BF16) for i in rng]
    a_ak = [jnp.where(strict4, ak[i][:CHUNK], 0.0).astype(BF16) for i in rng]
    a_rk = [jnp.where(incl4, ak[i][CHUNK:], 0.0).astype(BF16) for i in rng]

    x = a_ab
    tinv = [eye4 + x[i] for i in rng]
    x = [_dot(x[i].astype(BF16), bd(x[i].astype(BF16))) for i in rng]
    for _ in range(4):
        y = [_dot(jnp.concatenate([x[i], tinv[i]], axis=0).astype(BF16), bd(x[i].astype(BF16))) for i in rng]
        x = [y[i][:CHUNK] for i in rng]
        tinv = [tinv[i] + y[i][CHUNK:] for i in rng]
    tinv = [(tinv[i] + _dot(tinv[i].astype(BF16), bd(x[i].astype(BF16)))).astype(BF16) for i in rng]

    bdv = [bd(v_bf[i]) for i in rng]
    av = [_dot(jnp.concatenate([a_ak[i], a_rk[i]], axis=0), bdv[i]) for i in rng]
    wu = [_dot(tinv[i], bd2(at[i], av[i][:CHUNK])) for i in rng]
    wu_bf = [wu[i].astype(BF16) for i in rng]
    qo = [_dot(a_rb[i], bd2(wu[i][:, :GROUP], wu[i][:, GROUP:])) for i in rng]
    q = [(rt[i] + qo[i][:, :GROUP]).astype(BF16) for i in rng]
    o0 = [qo[i][:, GROUP:] + av[i][CHUNK:] for i in rng]
    zv = [jnp.concatenate([jnp.zeros_like(v_bf[i]), v_bf[i]], axis=1) for i in rng]
    mg = [_dot_tn(jnp.concatenate([bt[i], kt[i]], axis=0), jnp.concatenate([wu_bf[i], zv[i]], axis=0))
          for i in rng]
    qm = [jnp.concatenate([q[i], jnp.where(bdmask, mg[i][:, :GROUP], 0.0).astype(BF16)], axis=0) for i in rng]
    g_bd = [jnp.where(bdmask, mg[i][:, GROUP:], 0.0) for i in rng]
    return [(qm[i], o0[i], g_bd[i], decay_col[i], bonus[i]) for i in rng]


def _scan_body(reverse, has_prev, n_ctx_blk, blk_per_ctx, blk_per_lat, n_blk, *refs):
    if has_prev:
        (r_ref, k_ref, v_ref, lw_ref, al_ref, s0_ref, kk_ref, ka_ref, rk_ref, op_ref, bp_ref,
         o_ref, bon_ref, sfin_ref, h_ref) = refs
    else:
        (r_ref, k_ref, v_ref, lw_ref, al_ref, s0_ref, kk_ref, ka_ref, rk_ref,
         o_ref, bon_ref, sfin_ref, h_ref) = refs
        op_ref = bp_ref = None
    s = pl.program_id(1)
    j = (n_blk - 1 - s) if reverse else s
    is_ctx = j < n_ctx_blk
    first_c = (blk_per_ctx - 1) if reverse else 0
    first_l = (blk_per_lat - 1) if reverse else 0
    start = jnp.where(is_ctx, (j % blk_per_ctx) == first_c, ((j - n_ctx_blk) % blk_per_lat) == first_l)

    @pl.when(start)
    def _():
        h_ref[...] = jnp.where(is_ctx, 0.0, s0_ref[...])

    ones_mask = _head_ones()
    ones_bf = ones_mask.astype(F32).astype(BF16)
    t_i = lax.broadcasted_iota(jnp.int32, (CHUNK, CHUNK), 0)
    s_i = lax.broadcasted_iota(jnp.int32, (CHUNK, CHUNK), 1)
    tri = (s_i >= t_i) if reverse else (s_i <= t_i)
    tri_bf = tri.astype(F32).astype(BF16)
    t4 = lax.broadcasted_iota(jnp.int32, (CHUNK, GROUP), 0)
    s4 = lax.broadcasted_iota(jnp.int32, (CHUNK, GROUP), 1) % HEAD_DIM
    incl4 = (s4 >= t4) if reverse else (s4 <= t4)
    strict4 = (s4 > t4) if reverse else (s4 < t4)
    eye4 = (s4 == t4).astype(F32)
    consts = (ones_bf, ones_mask, tri_bf, strict4, incl4, eye4)

    n_chunks = SCAN_BLOCK // CHUNK
    order = list(range(n_chunks - 1, -1, -1) if reverse else range(n_chunks))
    sls = [slice(cidx * CHUNK, (cidx + 1) * CHUNK) for cidx in order]
    units = []
    for gi in range(SCAN_GROUPS):
        ln = slice(gi * GROUP, (gi + 1) * GROUP)
        units += [(r_ref[sl, ln], k_ref[sl, ln], v_ref[sl, ln], lw_ref[sl, ln], al_ref[sl, ln],
                   kk_ref[:, ln], ka_ref[:, ln], rk_ref[:, ln]) for sl in sls]
    flat = _scan_phase1(reverse, units, consts)
    pre = [flat[gi * n_chunks:(gi + 1) * n_chunks] for gi in range(SCAN_GROUPS)]

    h_state = [h_ref[gi] for gi in range(SCAN_GROUPS)]
    for ci, sl in enumerate(sls):
        for gi in range(SCAN_GROUPS):
            ln = slice(gi * GROUP, (gi + 1) * GROUP)
            qm, o0, g_bd, decay_col, bonus = pre[gi][ci]
            y = _dot(qm, h_state[gi].astype(BF16))
            o = y[:CHUNK] + o0
            h_state[gi] = decay_col * h_state[gi] + y[CHUNK:] + g_bd
            if has_prev:
                o = o + op_ref[sl, ln]
                bonus = bonus + bp_ref[sl, ln]
            o_ref[sl, ln] = o
            bon_ref[sl, ln] = bonus
    for gi in range(SCAN_GROUPS):
        h_ref[gi] = h_state[gi]
        sfin_ref[gi] = h_state[gi]


def _stage_scan(reverse, u, lw, al, s0_bd, k_k, k_a, r_k, prev, n_ctx_seq, ctx_len, n_lat_seq, lat_len):
    t = u.shape[0]
    blk_per_ctx = ctx_len // SCAN_BLOCK
    blk_per_lat = lat_len // SCAN_BLOCK
    n_ctx_blk = n_ctx_seq * blk_per_ctx
    n_blk = t // SCAN_BLOCK
    has_prev = prev is not None

    def jmap(s):
        return (n_blk - 1 - s) if reverse else s

    width = SCAN_GROUPS * GROUP

    def tok(colbase):
        return pl.BlockSpec((SCAN_BLOCK, width), lambda g, s: (jmap(s), colbase // width + g))

    def s0_map(g, s):
        j = jmap(s)
        b = jnp.clip((j - n_ctx_blk) // blk_per_lat, 0, n_lat_seq - 1)
        return (b, g, 0, 0)

    def sfin_map(g, s):
        j = jmap(s)
        b = jnp.where(j < n_ctx_blk, j // blk_per_ctx, n_ctx_seq)
        return (b, g, 0, 0)

    par = pl.BlockSpec((1, width), lambda g, s: (0, g))
    state_spec = lambda imap: pl.BlockSpec((None, SCAN_GROUPS, GROUP, GROUP), imap)
    in_specs = [tok(COL_R), tok(COL_K), tok(COL_V), tok(0), tok(0), state_spec(s0_map), par, par, par]
    args = [u, u, u, lw, al, s0_bd, k_k, k_a, r_k]
    if has_prev:
        in_specs += [tok(0), tok(0)]
        args += list(prev)
    out = jax.ShapeDtypeStruct((t, D_B), F32)
    body = functools.partial(_scan_body, reverse, has_prev, n_ctx_blk, blk_per_ctx, blk_per_lat, n_blk)
    return pl.pallas_call(
        body,
        out_shape=(out, out, jax.ShapeDtypeStruct((n_ctx_seq + 1, N_GROUPS, GROUP, GROUP), F32)),
        grid=(N_GROUPS // SCAN_GROUPS, n_blk),
        in_specs=in_specs,
        out_specs=(tok(0), tok(0), state_spec(sfin_map)),
        scratch_shapes=[pltpu.VMEM((SCAN_GROUPS, GROUP, GROUP), F32)],
        compiler_params=_cparams(("parallel", "arbitrary")),
        name="rwkv_scan_bwd" if reverse else "rwkv_scan_fwd",
    )(*args)


def _post_body(o_ref, bon_ref, lora_ref, lng_ref, lnb_ref, gup_ref, wo_ref, y_ref):
    ones_bf = _head_ones().astype(F32).astype(BF16)
    o = o_ref[...]
    mu = _headsum(o, ones_bf) * (1.0 / HEAD_DIM)
    d = o - mu
    var = _headsum(d * d, ones_bf) * (1.0 / HEAD_DIM)
    on = d * lax.rsqrt(var + GN_EPS) * lng_ref[...] + lnb_ref[...] + bon_ref[...]
    gd = lora_ref[:, 4 * LORA_PAD:4 * LORA_PAD + LORA_G]
    g = _dot(jax.nn.sigmoid(gd).astype(BF16), gup_ref[...])
    y_ref[...] = _dot((on * g).astype(BF16), wo_ref[...])


def _stage_post(o, bon, u, ln_g, ln_b, g_up_bf, w_o_bf):
    t = o.shape[0]
    tm = 256
    return pl.pallas_call(
        _post_body,
        out_shape=jax.ShapeDtypeStruct((t, D_MODEL), F32),
        grid=(t // tm,),
        in_specs=[pl.BlockSpec((tm, D_B), lambda i: (i, 0)),
                  pl.BlockSpec((tm, D_B), lambda i: (i, 0)),
                  pl.BlockSpec((tm, LORA_BLOCK), lambda i: (i, COL_LORA // LORA_BLOCK)),
                  pl.BlockSpec((1, D_B), lambda i: (0, 0)),
                  pl.BlockSpec((1, D_B), lambda i: (0, 0)),
                  pl.BlockSpec((LORA_G, D_B), lambda i: (0, 0)),
                  pl.BlockSpec((D_B, D_MODEL), lambda i: (0, 0))],
        out_specs=pl.BlockSpec((tm, D_MODEL), lambda i: (i, 0)),
        compiler_params=_cparams(("parallel",)),
        name="rwkv_post",
    )(o, bon, u, ln_g, ln_b, g_up_bf, w_o_bf)


def _merge_body(ya_ref, yb_ref, ga_ref, gb_ref, x_ref, gt_ref, sh_ref, sc_ref, g_ref, w_ref, x1_ref, h2_ref):
    m = jax.nn.sigmoid(ga_ref[...]) * ya_ref[...] + jax.nn.sigmoid(gb_ref[...]) * yb_ref[...]
    x1 = x_ref[...] + gt_ref[...] * _dot(m.astype(BF16), w_ref[...])
    x1_ref[...] = x1
    h2_ref[...] = _rms(x1) * g_ref[...] * (1.0 + sc_ref[...]) + sh_ref[...]


def _stage_merge(y_a, y_b, u, x_all, mod4, norm2_g, w_mix_bf, n_ctx_tok, lat_len):
    t = x_all.shape[0]
    tm = 256
    full = lambda col: pl.BlockSpec((tm, D_MODEL), lambda i: (i, col // D_MODEL))
    out = jax.ShapeDtypeStruct((t, D_MODEL), F32)
    return pl.pallas_call(
        _merge_body,
        out_shape=(out, out),
        grid=(t // tm,),
        in_specs=[full(0), full(0), full(COL_GA), full(COL_GB), full(0),
                  _mod_spec(2, tm, n_ctx_tok, lat_len),
                  _mod_spec(3, tm, n_ctx_tok, lat_len),
                  _mod_spec(4, tm, n_ctx_tok, lat_len),
                  pl.BlockSpec((1, D_MODEL), lambda i: (0, 0)),
                  pl.BlockSpec((D_MODEL, D_MODEL), lambda i: (0, 0))],
        out_specs=(full(0), full(0)),
        compiler_params=_cparams(("parallel",)),
        name="merge_norm2",
    )(y_a, y_b, u, u, x_all, mod4, mod4, mod4, norm2_g, w_mix_bf)


TOPK_TOKENS = 256


def _extract_topk(s, pos, limit, payload=None):
    vals, outs = [], []
    for _ in range(PEER_TOPK):
        m = jnp.max(s, axis=0, keepdims=True)
        pm = jnp.min(jnp.where(s == m, pos, limit), axis=0, keepdims=True)
        sel = pos == pm
        vals.append(m)
        if payload is None:
            outs.append(pm)
        else:
            outs.append(jnp.max(jnp.where(sel, payload, -1), axis=0, keepdims=True))
        s = jnp.where(sel, -jnp.inf, s)
    return jnp.concatenate(vals, axis=0), jnp.concatenate(outs, axis=0)


def _topk_body(h_ref, wq_ref, keys_ref, eid_ref, gate_ref, q_ref, eid_s, gate_s):
    q_ref[...] = _dot_nt(wq_ref[...], h_ref[...].astype(BF16))
    kpos = lax.broadcasted_iota(jnp.int32, (N_KEYS, TOPK_TOKENS), 0)
    half = PEER_TOPK // 2
    iota = lambda rows: lax.broadcasted_iota(jnp.int32, (rows, TOPK_TOKENS), 0)
    cpos = jnp.concatenate([iota(PEER_TOPK)] + [iota(half) + PEER_TOPK * i for i in range(1, half)]
                           + [(iota(half) + half) * PEER_TOPK], axis=0)

    def head(hd, carry):
        tops = []
        for p in range(2):
            off = pl.multiple_of(hd * (2 * HALF_KEY) + p * HALF_KEY, HALF_KEY)
            qs = q_ref[pl.ds(off, HALF_KEY), :].astype(BF16)
            s = _dot(keys_ref[p * PEER_HEADS + hd], qs)
            tops.append(_extract_topk(s, kpos, N_KEYS))
        (s1, i1), (s2, i2) = tops
        cand = jnp.concatenate([s1[0:1] + s2] + [s1[i:i + 1] + s2[:half] for i in range(1, half)]
                               + [s1[half:] + s2[0:1]], axis=0)
        ids = jnp.concatenate([i1[0:1] * N_KEYS + i2] + [i1[i:i + 1] * N_KEYS + i2[:half] for i in range(1, half)]
                              + [i1[half:] * N_KEYS + i2[0:1]], axis=0)
        tv, te = _extract_topk(cand, cpos, PEER_TOPK * PEER_TOPK, payload=ids)
        ex = jnp.exp(tv - tv[0:1])
        gate = ex / jnp.sum(ex, axis=0, keepdims=True)
        row = pl.multiple_of(hd * PEER_TOPK, PEER_TOPK)
        eid_s[pl.ds(row, PEER_TOPK), :] = te
        gate_s[pl.ds(row, PEER_TOPK), :] = gate
        return carry

    lax.fori_loop(0, PEER_HEADS, head, 0)
    eid_ref[...] = eid_s[...].T
    gate_ref[...] = gate_s[...].T


def _stage_topk(h2, w_q_t_bf, keys_bf):
    t = h2.shape[0]
    tb = TOPK_TOKENS
    return pl.pallas_call(
        _topk_body,
        out_shape=(jax.ShapeDtypeStruct((t, EXPERTS_PER_TOKEN), jnp.int32),
                   jax.ShapeDtypeStruct((t, EXPERTS_PER_TOKEN), F32)),
        grid=(t // tb,),
        in_specs=[pl.BlockSpec((tb, D_MODEL), lambda i: (i, 0)),
                  pl.BlockSpec((PEER_HEADS * 2 * HALF_KEY, D_MODEL), lambda i: (0, 0)),
                  pl.BlockSpec((2 * PEER_HEADS, N_KEYS, HALF_KEY), lambda i: (0, 0, 0))],
        out_specs=(pl.BlockSpec((tb, EXPERTS_PER_TOKEN), lambda i: (i, 0)),
                   pl.BlockSpec((tb, EXPERTS_PER_TOKEN), lambda i: (i, 0))),
        scratch_shapes=[pltpu.VMEM((PEER_HEADS * 2 * HALF_KEY, tb), F32),
                        pltpu.VMEM((EXPERTS_PER_TOKEN, tb), jnp.int32),
                        pltpu.VMEM((EXPERTS_PER_TOKEN, tb), F32)],
        compiler_params=_cparams(("parallel",)),
        name="peer_topk",
    )(h2, w_q_t_bf, keys_bf)


def _gather_copy(tab_ref, buf, sem, row, expert):
    return pltpu.make_async_copy(tab_ref.at[expert], buf.at[pl.ds(row, 1), :], sem)


def _gather_issue(tab_ref, id_ref, row0, buf, sem):
    for t in range(GATHER_TOKENS):
        for e in range(EXPERTS_PER_TOKEN):
            _gather_copy(tab_ref, buf, sem, t * EXPERTS_PER_TOKEN + e, id_ref[row0 + t, e]).start(priority=e % 2)


def _gather_wait(tab_ref, buf, sem):
    for r in range(GATHER_TOKENS * EXPERTS_PER_TOKEN):
        _gather_copy(tab_ref, buf, sem, r, 0).wait()


def _mix_tokens(buf, gate, h, eye):
    ne = EXPERTS_PER_TOKEN
    n_tiles = D_MODEL // LANE
    outs = []
    for t in range(GATHER_TOKENS):
        rows = slice(t * ne, (t + 1) * ne)
        acc = jnp.zeros((ne, LANE), F32)
        for j in range(n_tiles):
            w = buf[rows, j * LANE:(j + 1) * LANE]
            u = lax.bitcast_convert_type(w << 16, F32)
            acc = acc + u * h[t:t + 1, j * LANE:(j + 1) * LANE]
        pre = jnp.sum(acc, axis=-1, keepdims=True)
        gcol = jnp.sum(jnp.where(eye, gate[t:t + 1, :], 0.0), axis=-1, keepdims=True)
        coef = gcol * jax.nn.gelu(pre, approximate=True)
        cb = jnp.broadcast_to(coef, (ne, LANE))
        tiles = []
        for j in range(n_tiles):
            w = buf[rows, j * LANE:(j + 1) * LANE]
            v = lax.bitcast_convert_type(w & jnp.uint32(0xFFFF0000), F32)
            tiles.append(jnp.sum(cb * v, axis=0, keepdims=True))
        outs.append(jnp.concatenate(tiles, axis=1))
    return jnp.concatenate(outs, axis=0)


def _gather_body(ids_ref, idn_ref, gate_ref, h_ref, x_ref, gt_ref, fg_ref, tab_ref, o_ref,
                 buf0, buf1, sem):
    i = pl.program_id(0)
    n = pl.num_programs(0)
    tg = GATHER_TOKENS
    eye = (lax.broadcasted_iota(jnp.int32, (EXPERTS_PER_TOKEN, EXPERTS_PER_TOKEN), 0)
           == lax.broadcasted_iota(jnp.int32, (EXPERTS_PER_TOKEN, EXPERTS_PER_TOKEN), 1))

    @pl.when(i == 0)
    def _():
        _gather_issue(tab_ref, ids_ref, 0, buf0, sem.at[0])

    def finish(mix, lo):
        x2 = x_ref[lo:lo + tg, :] + gt_ref[...] * mix
        o_ref[lo:lo + tg, :] = _rms(x2) * fg_ref[...]

    _gather_wait(tab_ref, buf0, sem.at[0])
    _gather_issue(tab_ref, ids_ref, tg, buf1, sem.at[1])
    finish(_mix_tokens(buf0, gate_ref[0:tg, :], h_ref[0:tg, :], eye), 0)
    _gather_wait(tab_ref, buf1, sem.at[1])
    _gather_issue(tab_ref, idn_ref, 0, buf0, sem.at[0])
    finish(_mix_tokens(buf1, gate_ref[tg:2 * tg, :], h_ref[tg:2 * tg, :], eye), tg)

    @pl.when(i == n - 1)
    def _():
        _gather_wait(tab_ref, buf0, sem.at[0])


def _stage_gather(eid, gate, h2, x1, mod4, final_g, tab, n_ctx_tok, lat_len):
    t = h2.shape[0]
    tg = GATHER_TOKENS
    ts = 2 * tg
    n = t // ts
    tokspec = lambda w: pl.BlockSpec((ts, w), lambda i: (i, 0))
    rows = tg * EXPERTS_PER_TOKEN
    return pl.pallas_call(
        _gather_body,
        out_shape=jax.ShapeDtypeStruct((t, D_MODEL), F32),
        grid=(n,),
        in_specs=[pl.BlockSpec((ts, EXPERTS_PER_TOKEN), lambda i: (i, 0), memory_space=pltpu.SMEM),
                  pl.BlockSpec((tg, EXPERTS_PER_TOKEN), lambda i: (jnp.minimum(2 * i + 2, 2 * n - 1), 0),
                               memory_space=pltpu.SMEM),
                  tokspec(EXPERTS_PER_TOKEN), tokspec(D_MODEL), tokspec(D_MODEL),
                  _mod_spec(5, ts, n_ctx_tok, lat_len),
                  pl.BlockSpec((1, D_MODEL), lambda i: (0, 0)),
                  pl.BlockSpec(memory_space=pl.ANY)],
        out_specs=tokspec(D_MODEL),
        scratch_shapes=[pltpu.VMEM((rows, D_MODEL), jnp.uint32),
                        pltpu.VMEM((rows, D_MODEL), jnp.uint32),
                        pltpu.SemaphoreType.DMA((2,))],
        compiler_params=_cparams(("arbitrary",)),
        name="peer_gather",
    )(eid, eid, gate, h2, x1, mod4, final_g, tab)


def _pack_expert_tables(u_tab, v_tab):
    lo = lax.bitcast_convert_type(u_tab.astype(BF16), jnp.uint16).astype(jnp.uint32)
    hi = lax.bitcast_convert_type(v_tab.astype(BF16), jnp.uint16).astype(jnp.uint32)
    return (lo | (hi << 16))[:, None, :]


def _pad_rows(w, rows):
    return jnp.pad(w, ((0, 0), (0, rows - w.shape[1]), (0, 0)))


def _layer_weights(w_in, w_up, a_up):
    widths = (D_A, D_A, D_A, D_B, D_B, D_B, LORA_W, LORA_W, LORA_A, LORA_A, LORA_G, D_MODEL, D_MODEL)
    cuts = [0]
    for w in widths:
        cuts.append(cuts[-1] + w)
    seg = [w_in[:, cuts[i]:cuts[i + 1]] for i in range(len(widths))]
    (u_b, u_c, u_x, r, k, v, wd_f, wd_b, ad_f, ad_b, gd, gate_a, gate_b) = seg
    padc = lambda m: jnp.pad(m, ((0, 0), (0, LORA_PAD - m.shape[1])))
    tail = jnp.zeros((w_in.shape[0], LORA_BLOCK - 4 * LORA_PAD - LORA_G), w_in.dtype)
    w_cat = jnp.concatenate([r, k, v, gate_a, gate_b, u_b, u_c, u_x,
                             padc(wd_f), padc(wd_b), padc(ad_f), padc(ad_b), gd, tail], axis=1)
    return w_cat.astype(BF16), _pad_rows(w_up, LORA_PAD).astype(BF16), _pad_rows(a_up, LORA_PAD).astype(BF16)


def _state_to_blockdiag(s0):
    b = s0.shape[0]
    st = jnp.swapaxes(s0, -1, -2).reshape(b, N_GROUPS, HEADS_PER_GROUP, HEAD_DIM, HEAD_DIM)
    eye = jnp.eye(HEADS_PER_GROUP, dtype=s0.dtype)
    bd = jnp.einsum('bgikv,ij->bgikjv', st, eye)
    return bd.reshape(b, N_GROUPS, GROUP, GROUP)


def _blockdiag_to_state(bd):
    b = bd.shape[0]
    x = bd.reshape(b, N_GROUPS, HEADS_PER_GROUP, HEAD_DIM, HEADS_PER_GROUP, HEAD_DIM)
    d = jnp.diagonal(x, axis1=2, axis2=4)
    d = jnp.moveaxis(d, -1, 2)
    return jnp.swapaxes(d, -1, -2).reshape(b, H_B, HEAD_DIM, HEAD_DIM)


def kernel(x_prompt, x_sample, state_rwkv, c, c_ctx, w_ada, b_ada, norm1_g, norm2_g, w_in, conv_w, w_out_a, w0, w_up, a0, a_up, g_up, k_k, k_a, r_k, ln_g, ln_b, w_o, w_mix, w_q, sub_keys, u_tab, v_tab, final_g):
    n_ctx_seq, ctx_len, _ = x_prompt.shape
    n_lat_seq, lat_len, _ = x_sample.shape
    depth = w_in.shape[0]
    n_ctx_tok = n_ctx_seq * ctx_len
    assert depth == 1, "the final norm is fused into the last stage of a single layer"
    assert ctx_len % SCAN_BLOCK == 0 and lat_len % 512 == 0 and n_ctx_tok % 512 == 0
    assert lat_len % (GRID_W * (CONV_TILE // GRID_W)) == 0

    x_all = jnp.concatenate([x_prompt.reshape(n_ctx_tok, D_MODEL),
                             x_sample.reshape(n_lat_seq * lat_len, D_MODEL)], axis=0)
    mod_rows = 16
    cc = jnp.zeros((mod_rows, D_MODEL), F32).at[0].set(c_ctx).at[1:1 + n_lat_seq].set(c)
    new_states = []
    for l in range(depth):
        w_cat, w_up_bf, a_up_bf = _layer_weights(w_in[l], w_up[l], a_up[l])
        mod = _stage_mod(cc, w_ada[l].astype(BF16), b_ada[l][None, :])
        mod4 = mod.reshape(mod_rows, 6, 1, D_MODEL)
        u = _stage_inproj(x_all, mod4, norm1_g[l][None, :], w_cat, n_ctx_tok, lat_len)
        y_a = _stage_conv(u, conv_w[l], w_out_a[l].astype(BF16), n_ctx_tok, lat_len)
        lw_f, lw_b, al_f, al_b = _stage_rwkv_pre(u, w0[l], w_up_bf, a0[l], a_up_bf)
        kk2, ka2, rk2 = k_k[l][None, :], k_a[l][None, :], r_k[l].reshape(1, D_B)
        seq = (n_ctx_seq, ctx_len, n_lat_seq, lat_len)
        o_f, bon_f, sfin_f = _stage_scan(False, u, lw_f, al_f, _state_to_blockdiag(state_rwkv[:, l, 0]),
                                         kk2, ka2, rk2, None, *seq)
        o, bon, sfin_b = _stage_scan(True, u, lw_b, al_b, _state_to_blockdiag(state_rwkv[:, l, 1]),
                                     kk2, ka2, rk2, (o_f, bon_f), *seq)
        new_states.append(jnp.stack([_blockdiag_to_state(sfin_f[:n_ctx_seq]),
                                     _blockdiag_to_state(sfin_b[:n_ctx_seq])], axis=1))
        y_b = _stage_post(o, bon, u, ln_g[l][None, :], ln_b[l][None, :], g_up[l].astype(BF16), w_o[l].astype(BF16))
        x1, h2 = _stage_merge(y_a, y_b, u, x_all, mod4, norm2_g[l][None, :], w_mix[l].astype(BF16),
                              n_ctx_tok, lat_len)
        keys = sub_keys[l].reshape(2 * PEER_HEADS, N_KEYS, HALF_KEY).astype(BF16)
        eid, gate = _stage_topk(h2, w_q[l].T.astype(BF16), keys)
        tab = _pack_expert_tables(u_tab[l], v_tab[l])
        x_all = _stage_gather(eid, gate, h2, x1, mod4, final_g[None, :], tab, n_ctx_tok, lat_len)
    y_prompt = x_all[:n_ctx_tok].reshape(x_prompt.shape)
    y_sample = x_all[n_ctx_tok:].reshape(x_sample.shape)
    return (y_prompt, y_sample, jnp.stack(new_states, axis=1))
```

```python
import functools
import math

import jax
import jax.numpy as jnp
from jax import lax
from jax.experimental import pallas as pl
from jax.experimental.pallas import tpu as pltpu

F32 = jnp.float32
BF16 = jnp.bfloat16

D_MODEL = 2048
GRID_W = 64
D_A = 1024
H_B = 32
HEAD_DIM = 64
D_B = H_B * HEAD_DIM
LORA_W = 96
LORA_A = 96
LORA_G = 256
PEER_HEADS = 8
N_KEYS = 128
HALF_KEY = 128
PEER_TOPK = 16
RMS_EPS = 1e-6
GN_EPS = 64e-5

LANE = 128
CHUNK = 64
SCAN_BLOCK = 256
GROUP = 256
HEADS_PER_GROUP = GROUP // HEAD_DIM
N_GROUPS = D_B // GROUP
SCAN_GROUPS = 4
LORA_PAD = 128
EXPERTS_PER_TOKEN = PEER_HEADS * PEER_TOPK
GATHER_TOKENS = 8
VMEM_LIMIT = 56 * 1024 * 1024

COL_R, COL_K, COL_V, COL_GA, COL_GB = 0, 2048, 4096, 6144, 8192
COL_UB, COL_UC, COL_UX, COL_LORA = 10240, 11264, 12288, 13312
D_IN_PAD = 14336
LORA_BLOCK = 1024


def _cparams(sem, vmem=VMEM_LIMIT):
    return pltpu.CompilerParams(dimension_semantics=sem, vmem_limit_bytes=vmem)


def _dot(a, b):
    return jnp.dot(a, b, preferred_element_type=F32)


def _dot_nt(a, b):
    return lax.dot_general(a, b, (((1,), (1,)), ((), ())), preferred_element_type=F32)


def _dot_tn(a, b):
    return lax.dot_general(a, b, (((0,), (0,)), ((), ())), preferred_element_type=F32)


def _split(x, parts):
    out = []
    for _ in range(parts):
        p = x.astype(BF16)
        out.append(p)
        x = x - p.astype(F32)
    return out


def _head_ones():
    r = lax.broadcasted_iota(jnp.int32, (GROUP, GROUP), 0) // HEAD_DIM
    c = lax.broadcasted_iota(jnp.int32, (GROUP, GROUP), 1) // HEAD_DIM
    return r == c


def _headsum(x, ones_bf):
    outs = []
    for g in range(x.shape[1] // GROUP):
        xs = x[:, g * GROUP:(g + 1) * GROUP]
        hi, lo = _split(xs, 2)
        outs.append(_dot(hi, ones_bf) + _dot(lo, ones_bf))
    return outs[0] if len(outs) == 1 else jnp.concatenate(outs, axis=1)


def _rms(x):
    return x * lax.rsqrt(jnp.mean(x * x, axis=-1, keepdims=True) + RMS_EPS)


def _mod_body(c_ref, w_ref, b_ref, o_ref):
    c = c_ref[...]
    s = (c * jax.nn.sigmoid(c)).astype(BF16)
    o_ref[...] = _dot(s, w_ref[...]) + b_ref[...]


def _stage_mod(cc, w_ada_bf, b_ada):
    rows, d = cc.shape
    n = w_ada_bf.shape[1]
    tn = 2048
    return pl.pallas_call(
        _mod_body,
        out_shape=jax.ShapeDtypeStruct((rows, n), F32),
        grid=(n // tn,),
        in_specs=[pl.BlockSpec((rows, d), lambda j: (0, 0)),
                  pl.BlockSpec((d, tn), lambda j: (0, j)),
                  pl.BlockSpec((1, tn), lambda j: (0, j))],
        out_specs=pl.BlockSpec((rows, tn), lambda j: (0, j)),
        compiler_params=_cparams(("arbitrary",)),
        name="adaln_mod",
    )(cc, w_ada_bf, b_ada)


def _mod_spec(which, tm, n_ctx_tok, lat_len):
    nct = n_ctx_tok // tm
    per = lat_len // tm

    def imap(i, *_):
        row = jnp.where(i < nct, 0, 1 + (i - nct) // per)
        return (row, which, 0, 0)

    return pl.BlockSpec((None, None, 1, D_MODEL), imap)


def _inproj_body(x_ref, sh_ref, sc_ref, g_ref, w_ref, o_ref, h_ref):
    @pl.when(pl.program_id(1) == 0)
    def _():
        h = _rms(x_ref[...]) * g_ref[...] * (1.0 + sc_ref[...]) + sh_ref[...]
        h_ref[...] = h.astype(BF16)

    o_ref[...] = _dot(h_ref[...], w_ref[...])


def _stage_inproj(x_all, mod4, norm1_g, w_cat, n_ctx_tok, lat_len):
    t = x_all.shape[0]
    tm, tn = 1024, 1024
    return pl.pallas_call(
        _inproj_body,
        out_shape=jax.ShapeDtypeStruct((t, D_IN_PAD), F32),
        grid=(t // tm, D_IN_PAD // tn),
        in_specs=[pl.BlockSpec((tm, D_MODEL), lambda i, j: (i, 0)),
                  _mod_spec(0, tm, n_ctx_tok, lat_len),
                  _mod_spec(1, tm, n_ctx_tok, lat_len),
                  pl.BlockSpec((1, D_MODEL), lambda i, j: (0, 0)),
                  pl.BlockSpec((D_MODEL, tn), lambda i, j: (0, j))],
        out_specs=pl.BlockSpec((tm, tn), lambda i, j: (i, j)),
        scratch_shapes=[pltpu.VMEM((tm, D_MODEL), BF16)],
        compiler_params=_cparams(("parallel", "arbitrary")),
        name="norm1_inproj",
    )(x_all, mod4, mod4, norm1_g, w_cat)


CONV_TILE = 256


def _conv_body(n_ctx_tiles, tiles_per_lat, ub_ref, uc_ref, ux_ref, pc_ref, px_ref, nc_ref, nx_ref,
               cw_ref, wo_ref, y_ref):
    i = pl.program_id(0)
    is_lat = i >= n_ctx_tiles
    jl = (i - n_ctx_tiles) % tiles_per_lat
    half = D_A // 2
    z = uc_ref[...] * ux_ref[...]
    cw = cw_ref[...]
    row = lax.broadcasted_iota(jnp.int32, (CONV_TILE, 1), 0)
    pmask = jnp.where(is_lat, GRID_W - 1, CONV_TILE - 1)
    keep_p = (row & pmask) != 0
    keep_n = (row & pmask) != pmask
    zp = jnp.where(keep_p, pltpu.roll(z, 1, 0), 0.0)
    zn = jnp.where(keep_n, pltpu.roll(z, CONV_TILE - 1, 0), 0.0)
    conv_h = cw[0:1] * zp + cw[1:2] * z + cw[2:3] * zn
    zv = z[:, half:]
    hp = jnp.where(jl != 0, pc_ref[...] * px_ref[...], 0.0)
    hn = jnp.where(jl != tiles_per_lat - 1, nc_ref[...] * nx_ref[...], 0.0)
    vp = jnp.concatenate([hp, zv[:CONV_TILE - GRID_W]], axis=0)
    vn = jnp.concatenate([zv[GRID_W:], hn], axis=0)
    conv_v = cw[0:1, half:] * vp + cw[1:2, half:] * zv + cw[2:3, half:] * vn
    zc = jnp.concatenate([conv_h[:, :half], jnp.where(is_lat, conv_v, conv_h[:, half:])], axis=1)
    y_ref[...] = _dot((ub_ref[...] * zc).astype(BF16), wo_ref[...])


def _stage_conv(u, conv_w, w_out_a_bf, n_ctx_tok, lat_len):
    t = u.shape[0]
    tm = CONV_TILE
    n_ctx_tiles = n_ctx_tok // tm
    tiles_per_lat = lat_len // tm
    rpt = tm // GRID_W
    last_halo = t // GRID_W - 1
    half = D_A // 2
    cb = lambda col: col // D_A
    hb = lambda col: (col + half) // half
    body = functools.partial(_conv_body, n_ctx_tiles, tiles_per_lat)
    prev_map = lambda col: (lambda i: (jnp.maximum(i * rpt - 1, 0), hb(col)))
    next_map = lambda col: (lambda i: (jnp.minimum(i * rpt + rpt, last_halo), hb(col)))
    return pl.pallas_call(
        body,
        out_shape=jax.ShapeDtypeStruct((t, D_MODEL), F32),
        grid=(t // tm,),
        in_specs=[pl.BlockSpec((tm, D_A), lambda i: (i, cb(COL_UB))),
                  pl.BlockSpec((tm, D_A), lambda i: (i, cb(COL_UC))),
                  pl.BlockSpec((tm, D_A), lambda i: (i, cb(COL_UX))),
                  pl.BlockSpec((GRID_W, half), prev_map(COL_UC)),
                  pl.BlockSpec((GRID_W, half), prev_map(COL_UX)),
                  pl.BlockSpec((GRID_W, half), next_map(COL_UC)),
                  pl.BlockSpec((GRID_W, half), next_map(COL_UX)),
                  pl.BlockSpec((3, D_A), lambda i: (0, 0)),
                  pl.BlockSpec((D_A, D_MODEL), lambda i: (0, 0))],
        out_specs=pl.BlockSpec((tm, D_MODEL), lambda i: (i, 0)),
        compiler_params=_cparams(("parallel",)),
        name="short_conv",
    )(u, u, u, u, u, u, u, conv_w, w_out_a_bf)


def _scan_phase1(reverse, units, c):
    ones_bf, bdmask, tri_bf, strict4, incl4, eye4 = c
    n = len(units)
    rng = range(n)
    kk_p = [u[5] for u in units]
    ka_p = [u[6] for u in units]
    rk_p = [u[7] for u in units]

    def bd(x_bf):
        return jnp.concatenate([x_bf] * HEADS_PER_GROUP, axis=0) * ones_bf

    def bd2(x, y):
        return jnp.concatenate([bd(x.astype(BF16)), bd(y.astype(BF16))], axis=1)

    r = [u[0] for u in units]
    k = [u[1] for u in units]
    v = [u[2] for u in units]
    lw = [u[3] for u in units]
    al = [u[4] for u in units]
    kraw = [k[i] * kk_p[i] for i in rng]
    kd = [k[i] * (1.0 + (al[i] - 1.0) * ka_p[i]) for i in rng]
    hs = [_dot(jnp.concatenate(_split(kraw[i] * kraw[i], 2) + _split(r[i] * kd[i] * rk_p[i], 2), axis=0), ones_bf)
          for i in rng]
    kkn = [kraw[i] * lax.rsqrt(hs[i][:CHUNK] + hs[i][CHUNK:2 * CHUNK] + 1e-12) for i in rng]
    b = [kkn[i] * al[i] for i in rng]
    bonus = [(hs[i][2 * CHUNK:3 * CHUNK] + hs[i][3 * CHUNK:]) * v[i] for i in rng]

    parts = [_split(lw[i], 3) for i in rng]
    lc = [sum(_dot(tri_bf, p) for p in parts[i]) for i in rng]
    ltot = [(lc[i][0:1] if reverse else lc[i][CHUNK - 1:CHUNK]) for i in rng]
    decay_col = [jnp.broadcast_to(jnp.exp(ltot[i]), (GROUP, GROUP)).T for i in rng]
    rt = [r[i] * jnp.exp(lc[i]) for i in rng]
    at = [-kkn[i] * jnp.exp(lc[i] - lw[i]) for i in rng]
    einv = [jnp.exp(-lc[i]) for i in rng]
    bh = [b[i] * einv[i] for i in rng]
    kh = [kd[i] * einv[i] for i in rng]
    etail = [jnp.exp(ltot[i] - lc[i]) for i in rng]
    bt = [(b[i] * etail[i]).astype(BF16) for i in rng]
    kt = [(kd[i] * etail[i]).astype(BF16) for i in rng]
    v_bf = [v[i].astype(BF16) for i in rng]

    lhs1 = [jnp.concatenate([at[i], rt[i]], axis=0).astype(BF16) for i in rng]
    ab = [_dot_nt(lhs1[i], bd(bh[i].astype(BF16))) for i in rng]
    ak = [_dot_nt(lhs1[i], bd(kh[i].astype(BF16))) for i in rng]
    a_ab = [jnp.where(strict4, ab[i][:CHUNK], 0.0) for i in rng]
    a_rb = [jnp.where(incl4, ab[i][CHUNK:], 0.0).astype(BF16) for i in rng]
    a_ak = [jnp.where(strict4, ak[i][:CHUNK], 0.0).astype(BF16) for i in rng]
    a_rk = [jnp.where(incl4, ak[i][CHUNK:], 0.0).astype(BF16) for i in rng]

    x = a_ab
    tinv = [eye4 + x[i] for i in rng]
    x = [_dot(x[i].astype(BF16), bd(x[i].astype(BF16))) for i in rng]
    for _ in range(4):
        y = [_dot(jnp.concatenate([x[i], tinv[i]], axis=0).astype(BF16), bd(x[i].astype(BF16))) for i in rng]
        x = [y[i][:CHUNK] for i in rng]
        tinv = [tinv[i] + y[i][CHUNK:] for i in rng]
    tinv = [(tinv[i] + _dot(tinv[i].astype(BF16), bd(x[i].astype(BF16)))).astype(BF16) for i in rng]

    bdv = [bd(v_bf[i]) for i in rng]
    av = [_dot(jnp.concatenate([a_ak[i], a_rk[i]], axis=0), bdv[i]) for i in rng]
    wu = [_dot(tinv[i], bd2(at[i], av[i][:CHUNK])) for i in rng]
    wu_bf = [wu[i].astype(BF16) for i in rng]
    qo = [_dot(a_rb[i], bd2(wu[i][:, :GROUP], wu[i][:, GROUP:])) for i in rng]
    q = [(rt[i] + qo[i][:, :GROUP]).astype(BF16) for i in rng]
    o0 = [qo[i][:, GROUP:] + av[i][CHUNK:] for i in rng]
    zv = [jnp.concatenate([jnp.zeros_like(v_bf[i]), v_bf[i]], axis=1) for i in rng]
    mg = [_dot_tn(jnp.concatenate([bt[i], kt[i]], axis=0), jnp.concatenate([wu_bf[i], zv[i]], axis=0))
          for i in rng]
    qm = [jnp.concatenate([q[i], jnp.where(bdmask, mg[i][:, :GROUP], 0.0).astype(BF16)], axis=0) for i in rng]
    g_bd = [jnp.where(bdmask, mg[i][:, GROUP:], 0.0) for i in rng]
    return [(qm[i], o0[i], g_bd[i], decay_col[i], bonus[i]) for i in rng]


def _scan_body(reverse, has_prev, n_ctx_blk, blk_per_ctx, blk_per_lat, n_blk, *refs):
    if has_prev:
        (r_ref, k_ref, v_ref, wd_ref, ad_ref, w0_ref, wup_ref, a0_ref, aup_ref, s0_ref, kk_ref, ka_ref, rk_ref,
         op_ref, bp_ref, o_ref, bon_ref, sfin_ref, h_ref) = refs
    else:
        (r_ref, k_ref, v_ref, wd_ref, ad_ref, w0_ref, wup_ref, a0_ref, aup_ref, s0_ref, kk_ref, ka_ref, rk_ref,
         o_ref, bon_ref, sfin_ref, h_ref) = refs
        op_ref = bp_ref = None
    s = pl.program_id(1)
    j = (n_blk - 1 - s) if reverse else s
    is_ctx = j < n_ctx_blk
    first_c = (blk_per_ctx - 1) if reverse else 0
    first_l = (blk_per_lat - 1) if reverse else 0
    start = jnp.where(is_ctx, (j % blk_per_ctx) == first_c, ((j - n_ctx_blk) % blk_per_lat) == first_l)

    @pl.when(start)
    def _():
        h_ref[...] = jnp.where(is_ctx, 0.0, s0_ref[...])

    ones_mask = _head_ones()
    ones_bf = ones_mask.astype(F32).astype(BF16)
    t_i = lax.broadcasted_iota(jnp.int32, (CHUNK, CHUNK), 0)
    s_i = lax.broadcasted_iota(jnp.int32, (CHUNK, CHUNK), 1)
    tri = (s_i >= t_i) if reverse else (s_i <= t_i)
    tri_bf = tri.astype(F32).astype(BF16)
    t4 = lax.broadcasted_iota(jnp.int32, (CHUNK, GROUP), 0)
    s4 = lax.broadcasted_iota(jnp.int32, (CHUNK, GROUP), 1) % HEAD_DIM
    incl4 = (s4 >= t4) if reverse else (s4 <= t4)
    strict4 = (s4 > t4) if reverse else (s4 < t4)
    eye4 = (s4 == t4).astype(F32)
    consts = (ones_bf, ones_mask, tri_bf, strict4, incl4, eye4)

    xw = w0_ref[...] + _dot(jnp.tanh(wd_ref[...]).astype(BF16), wup_ref[...])
    lw_all = -math.exp(-0.5) * jax.nn.sigmoid(xw)
    al_all = jax.nn.sigmoid(a0_ref[...] + _dot(ad_ref[...].astype(BF16), aup_ref[...]))

    n_chunks = SCAN_BLOCK // CHUNK
    order = list(range(n_chunks - 1, -1, -1) if reverse else range(n_chunks))
    sls = [slice(cidx * CHUNK, (cidx + 1) * CHUNK) for cidx in order]
    units = []
    for gi in range(SCAN_GROUPS):
        ln = slice(gi * GROUP, (gi + 1) * GROUP)
        units += [(r_ref[sl, ln], k_ref[sl, ln], v_ref[sl, ln], lw_all[sl, ln], al_all[sl, ln],
                   kk_ref[:, ln], ka_ref[:, ln], rk_ref[:, ln]) for sl in sls]
    flat = _scan_phase1(reverse, units, consts)
    pre = [flat[gi * n_chunks:(gi + 1) * n_chunks] for gi in range(SCAN_GROUPS)]

    h_state = [h_ref[gi] for gi in range(SCAN_GROUPS)]
    for ci, sl in enumerate(sls):
        for gi in range(SCAN_GROUPS):
            ln = slice(gi * GROUP, (gi + 1) * GROUP)
            qm, o0, g_bd, decay_col, bonus = pre[gi][ci]
            y = _dot(qm, h_state[gi].astype(BF16))
            o = y[:CHUNK] + o0
            h_state[gi] = decay_col * h_state[gi] + y[CHUNK:] + g_bd
            if has_prev:
                o = o + op_ref[sl, ln]
                bonus = bonus + bp_ref[sl, ln]
            o_ref[sl, ln] = o
            bon_ref[sl, ln] = bonus
    for gi in range(SCAN_GROUPS):
        h_ref[gi] = h_state[gi]
        sfin_ref[gi] = h_state[gi]


def _stage_scan(reverse, u, w0, w_up_bf, a0, a_up_bf, s0_bd, k_k, k_a, r_k, prev,
                n_ctx_seq, ctx_len, n_lat_seq, lat_len):
    t = u.shape[0]
    blk_per_ctx = ctx_len // SCAN_BLOCK
    blk_per_lat = lat_len // SCAN_BLOCK
    n_ctx_blk = n_ctx_seq * blk_per_ctx
    n_blk = t // SCAN_BLOCK
    has_prev = prev is not None

    def jmap(s):
        return (n_blk - 1 - s) if reverse else s

    width = SCAN_GROUPS * GROUP

    def tok(colbase):
        return pl.BlockSpec((SCAN_BLOCK, width), lambda g, s: (jmap(s), colbase // width + g))

    def s0_map(g, s):
        j = jmap(s)
        b = jnp.clip((j - n_ctx_blk) // blk_per_lat, 0, n_lat_seq - 1)
        return (b, g, 0, 0)

    def sfin_map(g, s):
        j = jmap(s)
        b = jnp.where(j < n_ctx_blk, j // blk_per_ctx, n_ctx_seq)
        return (b, g, 0, 0)

    d = 1 if reverse else 0
    lora = lambda slot: pl.BlockSpec((SCAN_BLOCK, LORA_PAD), lambda g, s: (jmap(s), COL_LORA // LORA_PAD + slot))
    par = pl.BlockSpec((1, width), lambda g, s: (0, g))
    dpar = pl.BlockSpec((None, 1, width), lambda g, s: (d, 0, g))
    dup = pl.BlockSpec((None, LORA_PAD, width), lambda g, s: (d, 0, g))
    state_spec = lambda imap: pl.BlockSpec((None, SCAN_GROUPS, GROUP, GROUP), imap)
    in_specs = [tok(COL_R), tok(COL_K), tok(COL_V), lora(d), lora(2 + d), dpar, dup, dpar, dup,
                state_spec(s0_map), par, par, par]
    args = [u, u, u, u, u, w0[:, None, :], w_up_bf, a0[:, None, :], a_up_bf, s0_bd, k_k, k_a, r_k]
    if has_prev:
        in_specs += [tok(0), tok(0)]
        args += list(prev)
    out = jax.ShapeDtypeStruct((t, D_B), F32)
    body = functools.partial(_scan_body, reverse, has_prev, n_ctx_blk, blk_per_ctx, blk_per_lat, n_blk)
    return pl.pallas_call(
        body,
        out_shape=(out, out, jax.ShapeDtypeStruct((n_ctx_seq + 1, N_GROUPS, GROUP, GROUP), F32)),
        grid=(N_GROUPS // SCAN_GROUPS, n_blk),
        in_specs=in_specs,
        out_specs=(tok(0), tok(0), state_spec(sfin_map)),
        scratch_shapes=[pltpu.VMEM((SCAN_GROUPS, GROUP, GROUP), F32)],
        compiler_params=_cparams(("parallel", "arbitrary")),
        name="rwkv_scan_bwd" if reverse else "rwkv_scan_fwd",
    )(*args)


def _post_body(o_ref, bon_ref, lora_ref, lng_ref, lnb_ref, gup_ref, wo_ref, y_ref):
    ones_bf = _head_ones().astype(F32).astype(BF16)
    o = o_ref[...]
    mu = _headsum(o, ones_bf) * (1.0 / HEAD_DIM)
    d = o - mu
    var = _headsum(d * d, ones_bf) * (1.0 / HEAD_DIM)
    on = d * lax.rsqrt(var + GN_EPS) * lng_ref[...] + lnb_ref[...] + bon_ref[...]
    gd = lora_ref[:, 4 * LORA_PAD:4 * LORA_PAD + LORA_G]
    g = _dot(jax.nn.sigmoid(gd).astype(BF16), gup_ref[...])
    y_ref[...] = _dot((on * g).astype(BF16), wo_ref[...])


def _stage_post(o, bon, u, ln_g, ln_b, g_up_bf, w_o_bf):
    t = o.shape[0]
    tm = 256
    return pl.pallas_call(
        _post_body,
        out_shape=jax.ShapeDtypeStruct((t, D_MODEL), F32),
        grid=(t // tm,),
        in_specs=[pl.BlockSpec((tm, D_B), lambda i: (i, 0)),
                  pl.BlockSpec((tm, D_B), lambda i: (i, 0)),
                  pl.BlockSpec((tm, LORA_BLOCK), lambda i: (i, COL_LORA // LORA_BLOCK)),
                  pl.BlockSpec((1, D_B), lambda i: (0, 0)),
                  pl.BlockSpec((1, D_B), lambda i: (0, 0)),
                  pl.BlockSpec((LORA_G, D_B), lambda i: (0, 0)),
                  pl.BlockSpec((D_B, D_MODEL), lambda i: (0, 0))],
        out_specs=pl.BlockSpec((tm, D_MODEL), lambda i: (i, 0)),
        compiler_params=_cparams(("parallel",)),
        name="rwkv_post",
    )(o, bon, u, ln_g, ln_b, g_up_bf, w_o_bf)


def _merge_body(ya_ref, yb_ref, ga_ref, gb_ref, x_ref, gt_ref, sh_ref, sc_ref, g_ref, w_ref, x1_ref, h2_ref):
    m = jax.nn.sigmoid(ga_ref[...]) * ya_ref[...] + jax.nn.sigmoid(gb_ref[...]) * yb_ref[...]
    x1 = x_ref[...] + gt_ref[...] * _dot(m.astype(BF16), w_ref[...])
    x1_ref[...] = x1
    h2_ref[...] = _rms(x1) * g_ref[...] * (1.0 + sc_ref[...]) + sh_ref[...]


def _stage_merge(y_a, y_b, u, x_all, mod4, norm2_g, w_mix_bf, n_ctx_tok, lat_len):
    t = x_all.shape[0]
    tm = 256
    full = lambda col: pl.BlockSpec((tm, D_MODEL), lambda i: (i, col // D_MODEL))
    out = jax.ShapeDtypeStruct((t, D_MODEL), F32)
    return pl.pallas_call(
        _merge_body,
        out_shape=(out, out),
        grid=(t // tm,),
        in_specs=[full(0), full(0), full(COL_GA), full(COL_GB), full(0),
                  _mod_spec(2, tm, n_ctx_tok, lat_len),
                  _mod_spec(3, tm, n_ctx_tok, lat_len),
                  _mod_spec(4, tm, n_ctx_tok, lat_len),
                  pl.BlockSpec((1, D_MODEL), lambda i: (0, 0)),
                  pl.BlockSpec((D_MODEL, D_MODEL), lambda i: (0, 0))],
        out_specs=(full(0), full(0)),
        compiler_params=_cparams(("parallel",)),
        name="merge_norm2",
    )(y_a, y_b, u, u, x_all, mod4, mod4, mod4, norm2_g, w_mix_bf)


TOPK_TOKENS = 256


def _extract_topk(s, pos, limit, payload=None):
    vals, outs = [], []
    for _ in range(PEER_TOPK):
        m = jnp.max(s, axis=0, keepdims=True)
        pm = jnp.min(jnp.where(s == m, pos, limit), axis=0, keepdims=True)
        sel = pos == pm
        vals.append(m)
        if payload is None:
            outs.append(pm)
        else:
            outs.append(jnp.max(jnp.where(sel, payload, -1), axis=0, keepdims=True))
        s = jnp.where(sel, -jnp.inf, s)
    return jnp.concatenate(vals, axis=0), jnp.concatenate(outs, axis=0)


def _topk_body(h_ref, wq_ref, keys_ref, eid_ref, gate_ref, q_ref, eid_s, gate_s):
    q_ref[...] = _dot_nt(wq_ref[...], h_ref[...].astype(BF16))
    kpos = lax.broadcasted_iota(jnp.int32, (N_KEYS, TOPK_TOKENS), 0)
    half = PEER_TOPK // 2
    iota = lambda rows: lax.broadcasted_iota(jnp.int32, (rows, TOPK_TOKENS), 0)
    cpos = jnp.concatenate([iota(PEER_TOPK)] + [iota(half) + PEER_TOPK * i for i in range(1, half)]
                           + [(iota(half) + half) * PEER_TOPK], axis=0)

    def head(hd, carry):
        tops = []
        for p in range(2):
            off = pl.multiple_of(hd * (2 * HALF_KEY) + p * HALF_KEY, HALF_KEY)
            qs = q_ref[pl.ds(off, HALF_KEY), :].astype(BF16)
            s = _dot(keys_ref[p * PEER_HEADS + hd], qs)
            tops.append(_extract_topk(s, kpos, N_KEYS))
        (s1, i1), (s2, i2) = tops
        cand = jnp.concatenate([s1[0:1] + s2] + [s1[i:i + 1] + s2[:half] for i in range(1, half)]
                               + [s1[half:] + s2[0:1]], axis=0)
        ids = jnp.concatenate([i1[0:1] * N_KEYS + i2] + [i1[i:i + 1] * N_KEYS + i2[:half] for i in range(1, half)]
                              + [i1[half:] * N_KEYS + i2[0:1]], axis=0)
        tv, te = _extract_topk(cand, cpos, PEER_TOPK * PEER_TOPK, payload=ids)
        ex = jnp.exp(tv - tv[0:1])
        gate = ex / jnp.sum(ex, axis=0, keepdims=True)
        row = pl.multiple_of(hd * PEER_TOPK, PEER_TOPK)
        eid_s[pl.ds(row, PEER_TOPK), :] = te
        gate_s[pl.ds(row, PEER_TOPK), :] = gate
        return carry

    lax.fori_loop(0, PEER_HEADS, head, 0)
    eid_ref[...] = eid_s[...].T
    gate_ref[...] = gate_s[...].T


def _stage_topk(h2, w_q_t_bf, keys_bf):
    t = h2.shape[0]
    tb = TOPK_TOKENS
    return pl.pallas_call(
        _topk_body,
        out_shape=(jax.ShapeDtypeStruct((t, EXPERTS_PER_TOKEN), jnp.int32),
                   jax.ShapeDtypeStruct((t, EXPERTS_PER_TOKEN), F32)),
        grid=(t // tb,),
        in_specs=[pl.BlockSpec((tb, D_MODEL), lambda i: (i, 0)),
                  pl.BlockSpec((PEER_HEADS * 2 * HALF_KEY, D_MODEL), lambda i: (0, 0)),
                  pl.BlockSpec((2 * PEER_HEADS, N_KEYS, HALF_KEY), lambda i: (0, 0, 0))],
        out_specs=(pl.BlockSpec((tb, EXPERTS_PER_TOKEN), lambda i: (i, 0)),
                   pl.BlockSpec((tb, EXPERTS_PER_TOKEN), lambda i: (i, 0))),
        scratch_shapes=[pltpu.VMEM((PEER_HEADS * 2 * HALF_KEY, tb), F32),
                        pltpu.VMEM((EXPERTS_PER_TOKEN, tb), jnp.int32),
                        pltpu.VMEM((EXPERTS_PER_TOKEN, tb), F32)],
        compiler_params=_cparams(("parallel",)),
        name="peer_topk",
    )(h2, w_q_t_bf, keys_bf)


def _gather_copy(tab_ref, buf, sem, row, expert):
    return pltpu.make_async_copy(tab_ref.at[expert], buf.at[pl.ds(row, 1), :], sem)


def _gather_issue(tab_ref, id_ref, row0, buf, sem):
    for t in range(GATHER_TOKENS):
        for e in range(EXPERTS_PER_TOKEN):
            _gather_copy(tab_ref, buf, sem, t * EXPERTS_PER_TOKEN + e, id_ref[row0 + t, e]).start(priority=e % 2)


def _gather_wait(tab_ref, buf, sem):
    for r in range(GATHER_TOKENS * EXPERTS_PER_TOKEN):
        _gather_copy(tab_ref, buf, sem, r, 0).wait()


def _mix_tokens(buf, gate, h, eye):
    ne = EXPERTS_PER_TOKEN
    n_tiles = D_MODEL // LANE
    outs = []
    for t in range(GATHER_TOKENS):
        rows = slice(t * ne, (t + 1) * ne)
        acc = jnp.zeros((ne, LANE), F32)
        for j in range(n_tiles):
            w = buf[rows, j * LANE:(j + 1) * LANE]
            u = lax.bitcast_convert_type(w << 16, F32)
            acc = acc + u * h[t:t + 1, j * LANE:(j + 1) * LANE]
        pre = jnp.sum(acc, axis=-1, keepdims=True)
        gcol = jnp.sum(jnp.where(eye, gate[t:t + 1, :], 0.0), axis=-1, keepdims=True)
        coef = gcol * jax.nn.gelu(pre, approximate=True)
        cb = jnp.broadcast_to(coef, (ne, LANE))
        tiles = []
        for j in range(n_tiles):
            w = buf[rows, j * LANE:(j + 1) * LANE]
            v = lax.bitcast_convert_type(w & jnp.uint32(0xFFFF0000), F32)
            tiles.append(jnp.sum(cb * v, axis=0, keepdims=True))
        outs.append(jnp.concatenate(tiles, axis=1))
    return jnp.concatenate(outs, axis=0)


def _gather_body(ids_ref, idn_ref, gate_ref, h_ref, x_ref, gt_ref, fg_ref, tab_ref, o_ref,
                 buf0, buf1, sem):
    i = pl.program_id(0)
    n = pl.num_programs(0)
    tg = GATHER_TOKENS
    eye = (lax.broadcasted_iota(jnp.int32, (EXPERTS_PER_TOKEN, EXPERTS_PER_TOKEN), 0)
           == lax.broadcasted_iota(jnp.int32, (EXPERTS_PER_TOKEN, EXPERTS_PER_TOKEN), 1))

    @pl.when(i == 0)
    def _():
        _gather_issue(tab_ref, ids_ref, 0, buf0, sem.at[0])

    def finish(mix, lo):
        x2 = x_ref[lo:lo + tg, :] + gt_ref[...] * mix
        o_ref[lo:lo + tg, :] = _rms(x2) * fg_ref[...]

    _gather_wait(tab_ref, buf0, sem.at[0])
    _gather_issue(tab_ref, ids_ref, tg, buf1, sem.at[1])
    finish(_mix_tokens(buf0, gate_ref[0:tg, :], h_ref[0:tg, :], eye), 0)
    _gather_wait(tab_ref, buf1, sem.at[1])
    _gather_issue(tab_ref, idn_ref, 0, buf0, sem.at[0])
    finish(_mix_tokens(buf1, gate_ref[tg:2 * tg, :], h_ref[tg:2 * tg, :], eye), tg)

    @pl.when(i == n - 1)
    def _():
        _gather_wait(tab_ref, buf0, sem.at[0])


def _stage_gather(eid, gate, h2, x1, mod4, final_g, tab, n_ctx_tok, lat_len):
    t = h2.shape[0]
    tg = GATHER_TOKENS
    ts = 2 * tg
    n = t // ts
    tokspec = lambda w: pl.BlockSpec((ts, w), lambda i: (i, 0))
    rows = tg * EXPERTS_PER_TOKEN
    return pl.pallas_call(
        _gather_body,
        out_shape=jax.ShapeDtypeStruct((t, D_MODEL), F32),
        grid=(n,),
        in_specs=[pl.BlockSpec((ts, EXPERTS_PER_TOKEN), lambda i: (i, 0), memory_space=pltpu.SMEM),
                  pl.BlockSpec((tg, EXPERTS_PER_TOKEN), lambda i: (jnp.minimum(2 * i + 2, 2 * n - 1), 0),
                               memory_space=pltpu.SMEM),
                  tokspec(EXPERTS_PER_TOKEN), tokspec(D_MODEL), tokspec(D_MODEL),
                  _mod_spec(5, ts, n_ctx_tok, lat_len),
                  pl.BlockSpec((1, D_MODEL), lambda i: (0, 0)),
                  pl.BlockSpec(memory_space=pl.ANY)],
        out_specs=tokspec(D_MODEL),
        scratch_shapes=[pltpu.VMEM((rows, D_MODEL), jnp.uint32),
                        pltpu.VMEM((rows, D_MODEL), jnp.uint32),
                        pltpu.SemaphoreType.DMA((2,))],
        compiler_params=_cparams(("arbitrary",)),
        name="peer_gather",
    )(eid, eid, gate, h2, x1, mod4, final_g, tab)


def _pack_expert_tables(u_tab, v_tab):
    lo = lax.bitcast_convert_type(u_tab.astype(BF16), jnp.uint16).astype(jnp.uint32)
    hi = lax.bitcast_convert_type(v_tab.astype(BF16), jnp.uint16).astype(jnp.uint32)
    return (lo | (hi << 16))[:, None, :]


def _pad_rows(w, rows):
    return jnp.pad(w, ((0, 0), (0, rows - w.shape[1]), (0, 0)))


def _layer_weights(w_in, w_up, a_up):
    widths = (D_A, D_A, D_A, D_B, D_B, D_B, LORA_W, LORA_W, LORA_A, LORA_A, LORA_G, D_MODEL, D_MODEL)
    cuts = [0]
    for w in widths:
        cuts.append(cuts[-1] + w)
    seg = [w_in[:, cuts[i]:cuts[i + 1]] for i in range(len(widths))]
    (u_b, u_c, u_x, r, k, v, wd_f, wd_b, ad_f, ad_b, gd, gate_a, gate_b) = seg
    padc = lambda m: jnp.pad(m, ((0, 0), (0, LORA_PAD - m.shape[1])))
    tail = jnp.zeros((w_in.shape[0], LORA_BLOCK - 4 * LORA_PAD - LORA_G), w_in.dtype)
    w_cat = jnp.concatenate([r, k, v, gate_a, gate_b, u_b, u_c, u_x,
                             padc(wd_f), padc(wd_b), padc(ad_f), padc(ad_b), gd, tail], axis=1)
    return w_cat.astype(BF16), _pad_rows(w_up, LORA_PAD).astype(BF16), _pad_rows(a_up, LORA_PAD).astype(BF16)


def _state_to_blockdiag(s0):
    b = s0.shape[0]
    st = jnp.swapaxes(s0, -1, -2).reshape(b, N_GROUPS, HEADS_PER_GROUP, HEAD_DIM, HEAD_DIM)
    eye = jnp.eye(HEADS_PER_GROUP, dtype=s0.dtype)
    bd = jnp.einsum('bgikv,ij->bgikjv', st, eye)
    return bd.reshape(b, N_GROUPS, GROUP, GROUP)


def _blockdiag_to_state(bd):
    b = bd.shape[0]
    x = bd.reshape(b, N_GROUPS, HEADS_PER_GROUP, HEAD_DIM, HEADS_PER_GROUP, HEAD_DIM)
    d = jnp.diagonal(x, axis1=2, axis2=4)
    d = jnp.moveaxis(d, -1, 2)
    return jnp.swapaxes(d, -1, -2).reshape(b, H_B, HEAD_DIM, HEAD_DIM)


def kernel(x_prompt, x_sample, state_rwkv, c, c_ctx, w_ada, b_ada, norm1_g, norm2_g, w_in, conv_w, w_out_a, w0, w_up, a0, a_up, g_up, k_k, k_a, r_k, ln_g, ln_b, w_o, w_mix, w_q, sub_keys, u_tab, v_tab, final_g):
    n_ctx_seq, ctx_len, _ = x_prompt.shape
    n_lat_seq, lat_len, _ = x_sample.shape
    depth = w_in.shape[0]
    n_ctx_tok = n_ctx_seq * ctx_len
    assert depth == 1, "the final norm is fused into the last stage of a single layer"
    assert ctx_len % SCAN_BLOCK == 0 and lat_len % 1024 == 0 and n_ctx_tok % 1024 == 0
    assert lat_len % (GRID_W * (CONV_TILE // GRID_W)) == 0

    x_all = jnp.concatenate([x_prompt.reshape(n_ctx_tok, D_MODEL),
                             x_sample.reshape(n_lat_seq * lat_len, D_MODEL)], axis=0)
    mod_rows = 16
    cc = jnp.zeros((mod_rows, D_MODEL), F32).at[0].set(c_ctx).at[1:1 + n_lat_seq].set(c)
    new_states = []
    for l in range(depth):
        w_cat, w_up_bf, a_up_bf = _layer_weights(w_in[l], w_up[l], a_up[l])
        mod = _stage_mod(cc, w_ada[l].astype(BF16), b_ada[l][None, :])
        mod4 = mod.reshape(mod_rows, 6, 1, D_MODEL)
        u = _stage_inproj(x_all, mod4, norm1_g[l][None, :], w_cat, n_ctx_tok, lat_len)
        y_a = _stage_conv(u, conv_w[l], w_out_a[l].astype(BF16), n_ctx_tok, lat_len)
        kk2, ka2, rk2 = k_k[l][None, :], k_a[l][None, :], r_k[l].reshape(1, D_B)
        seq = (n_ctx_seq, ctx_len, n_lat_seq, lat_len)
        lowrank = (w0[l], w_up_bf, a0[l], a_up_bf)
        o_f, bon_f, sfin_f = _stage_scan(False, u, *lowrank, _state_to_blockdiag(state_rwkv[:, l, 0]),
                                         kk2, ka2, rk2, None, *seq)
        o, bon, sfin_b = _stage_scan(True, u, *lowrank, _state_to_blockdiag(state_rwkv[:, l, 1]),
                                     kk2, ka2, rk2, (o_f, bon_f), *seq)
        new_states.append(jnp.stack([_blockdiag_to_state(sfin_f[:n_ctx_seq]),
                                     _blockdiag_to_state(sfin_b[:n_ctx_seq])], axis=1))
        y_b = _stage_post(o, bon, u, ln_g[l][None, :], ln_b[l][None, :], g_up[l].astype(BF16), w_o[l].astype(BF16))
        x1, h2 = _stage_merge(y_a, y_b, u, x_all, mod4, norm2_g[l][None, :], w_mix[l].astype(BF16),
                              n_ctx_tok, lat_len)
        keys = sub_keys[l].reshape(2 * PEER_HEADS, N_KEYS, HALF_KEY).astype(BF16)
        eid, gate = _stage_topk(h2, w_q[l].T.astype(BF16), keys)
        tab = _pack_expert_tables(u_tab[l], v_tab[l])
        x_all = _stage_gather(eid, gate, h2, x1, mod4, final_g[None, :], tab, n_ctx_tok, lat_len)
    y_prompt = x_all[:n_ctx_tok].reshape(x_prompt.shape)
    y_sample = x_all[n_ctx_tok:].reshape(x_sample.shape)
    return (y_prompt, y_sample, jnp.stack(new_states, axis=1))
```

```python
import functools
import math

import jax
import jax.numpy as jnp
from jax import lax
from jax.experimental import pallas as pl
from jax.experimental.pallas import tpu as pltpu

F32 = jnp.float32
BF16 = jnp.bfloat16

D_MODEL = 2048
GRID_W = 64
D_A = 1024
H_B = 32
HEAD_DIM = 64
D_B = H_B * HEAD_DIM
LORA_W = 96
LORA_A = 96
LORA_G = 256
PEER_HEADS = 8
N_KEYS = 128
HALF_KEY = 128
PEER_TOPK = 16
RMS_EPS = 1e-6
GN_EPS = 64e-5

LANE = 128
CHUNK = 64
SCAN_BLOCK = 256
GROUP = 256
HEADS_PER_GROUP = GROUP // HEAD_DIM
N_GROUPS = D_B // GROUP
SCAN_GROUPS = 4
LORA_PAD = 128
EXPERTS_PER_TOKEN = PEER_HEADS * PEER_TOPK
GATHER_TOKENS = 8
VMEM_LIMIT = 56 * 1024 * 1024

COL_R, COL_K, COL_V, COL_GA, COL_GB = 0, 2048, 4096, 6144, 8192
COL_UB, COL_UC, COL_UX, COL_LORA = 10240, 11264, 12288, 13312
D_IN_PAD = 14336
LORA_BLOCK = 1024


def _cparams(sem, vmem=VMEM_LIMIT):
    return pltpu.CompilerParams(dimension_semantics=sem, vmem_limit_bytes=vmem)


def _dot(a, b):
    return jnp.dot(a, b, preferred_element_type=F32)


def _dot_nt(a, b):
    return lax.dot_general(a, b, (((1,), (1,)), ((), ())), preferred_element_type=F32)


def _dot_tn(a, b):
    return lax.dot_general(a, b, (((0,), (0,)), ((), ())), preferred_element_type=F32)


def _split(x, parts):
    out = []
    for _ in range(parts):
        p = x.astype(BF16)
        out.append(p)
        x = x - p.astype(F32)
    return out


def _head_ones():
    r = lax.broadcasted_iota(jnp.int32, (GROUP, GROUP), 0) // HEAD_DIM
    c = lax.broadcasted_iota(jnp.int32, (GROUP, GROUP), 1) // HEAD_DIM
    return r == c


def _headsum(x, ones_bf):
    outs = []
    for g in range(x.shape[1] // GROUP):
        xs = x[:, g * GROUP:(g + 1) * GROUP]
        hi, lo = _split(xs, 2)
        outs.append(_dot(hi, ones_bf) + _dot(lo, ones_bf))
    return outs[0] if len(outs) == 1 else jnp.concatenate(outs, axis=1)


def _rms(x):
    return x * lax.rsqrt(jnp.mean(x * x, axis=-1, keepdims=True) + RMS_EPS)


def _mod_body(c_ref, w_ref, b_ref, o_ref):
    c = c_ref[...]
    s = (c * jax.nn.sigmoid(c)).astype(BF16)
    o_ref[...] = _dot(s, w_ref[...]) + b_ref[...]


def _stage_mod(cc, w_ada_bf, b_ada):
    rows, d = cc.shape
    n = w_ada_bf.shape[1]
    tn = 2048
    return pl.pallas_call(
        _mod_body,
        out_shape=jax.ShapeDtypeStruct((rows, n), F32),
        grid=(n // tn,),
        in_specs=[pl.BlockSpec((rows, d), lambda j: (0, 0)),
                  pl.BlockSpec((d, tn), lambda j: (0, j)),
                  pl.BlockSpec((1, tn), lambda j: (0, j))],
        out_specs=pl.BlockSpec((rows, tn), lambda j: (0, j)),
        compiler_params=_cparams(("arbitrary",)),
        name="adaln_mod",
    )(cc, w_ada_bf, b_ada)


def _mod_spec(which, tm, n_ctx_tok, lat_len):
    nct = n_ctx_tok // tm
    per = lat_len // tm

    def imap(i, *_):
        row = jnp.where(i < nct, 0, 1 + (i - nct) // per)
        return (row, which, 0, 0)

    return pl.BlockSpec((None, None, 1, D_MODEL), imap)


def _inproj_body(x_ref, sh_ref, sc_ref, g_ref, w_ref, o_ref, h_ref):
    @pl.when(pl.program_id(1) == 0)
    def _():
        h = _rms(x_ref[...]) * g_ref[...] * (1.0 + sc_ref[...]) + sh_ref[...]
        h_ref[...] = h.astype(BF16)

    o_ref[...] = _dot(h_ref[...], w_ref[...])


def _stage_inproj(x_all, mod4, norm1_g, w_cat, n_ctx_tok, lat_len):
    t = x_all.shape[0]
    tm, tn = 1024, 1024
    return pl.pallas_call(
        _inproj_body,
        out_shape=jax.ShapeDtypeStruct((t, D_IN_PAD), F32),
        grid=(t // tm, D_IN_PAD // tn),
        in_specs=[pl.BlockSpec((tm, D_MODEL), lambda i, j: (i, 0)),
                  _mod_spec(0, tm, n_ctx_tok, lat_len),
                  _mod_spec(1, tm, n_ctx_tok, lat_len),
                  pl.BlockSpec((1, D_MODEL), lambda i, j: (0, 0)),
                  pl.BlockSpec((D_MODEL, tn), lambda i, j: (0, j))],
        out_specs=pl.BlockSpec((tm, tn), lambda i, j: (i, j)),
        scratch_shapes=[pltpu.VMEM((tm, D_MODEL), BF16)],
        compiler_params=_cparams(("parallel", "arbitrary")),
        name="norm1_inproj",
    )(x_all, mod4, mod4, norm1_g, w_cat)


CONV_TILE = 256


def _conv_body(n_ctx_tiles, tiles_per_lat, ub_ref, uc_ref, ux_ref, pc_ref, px_ref, nc_ref, nx_ref,
               cw_ref, wo_ref, y_ref):
    i = pl.program_id(0)
    is_lat = i >= n_ctx_tiles
    jl = (i - n_ctx_tiles) % tiles_per_lat
    half = D_A // 2
    z = uc_ref[...] * ux_ref[...]
    cw = cw_ref[...]
    row = lax.broadcasted_iota(jnp.int32, (CONV_TILE, 1), 0)
    pmask = jnp.where(is_lat, GRID_W - 1, CONV_TILE - 1)
    keep_p = (row & pmask) != 0
    keep_n = (row & pmask) != pmask
    zp = jnp.where(keep_p, pltpu.roll(z, 1, 0), 0.0)
    zn = jnp.where(keep_n, pltpu.roll(z, CONV_TILE - 1, 0), 0.0)
    conv_h = cw[0:1] * zp + cw[1:2] * z + cw[2:3] * zn
    zv = z[:, half:]
    hp = jnp.where(jl != 0, pc_ref[...] * px_ref[...], 0.0)
    hn = jnp.where(jl != tiles_per_lat - 1, nc_ref[...] * nx_ref[...], 0.0)
    vp = jnp.concatenate([hp, zv[:CONV_TILE - GRID_W]], axis=0)
    vn = jnp.concatenate([zv[GRID_W:], hn], axis=0)
    conv_v = cw[0:1, half:] * vp + cw[1:2, half:] * zv + cw[2:3, half:] * vn
    zc = jnp.concatenate([conv_h[:, :half], jnp.where(is_lat, conv_v, conv_h[:, half:])], axis=1)
    y_ref[...] = _dot((ub_ref[...] * zc).astype(BF16), wo_ref[...])


def _stage_conv(u, conv_w, w_out_a_bf, n_ctx_tok, lat_len):
    t = u.shape[0]
    tm = CONV_TILE
    n_ctx_tiles = n_ctx_tok // tm
    tiles_per_lat = lat_len // tm
    rpt = tm // GRID_W
    last_halo = t // GRID_W - 1
    half = D_A // 2
    cb = lambda col: col // D_A
    hb = lambda col: (col + half) // half
    body = functools.partial(_conv_body, n_ctx_tiles, tiles_per_lat)
    prev_map = lambda col: (lambda i: (jnp.maximum(i * rpt - 1, 0), hb(col)))
    next_map = lambda col: (lambda i: (jnp.minimum(i * rpt + rpt, last_halo), hb(col)))
    return pl.pallas_call(
        body,
        out_shape=jax.ShapeDtypeStruct((t, D_MODEL), F32),
        grid=(t // tm,),
        in_specs=[pl.BlockSpec((tm, D_A), lambda i: (i, cb(COL_UB))),
                  pl.BlockSpec((tm, D_A), lambda i: (i, cb(COL_UC))),
                  pl.BlockSpec((tm, D_A), lambda i: (i, cb(COL_UX))),
                  pl.BlockSpec((GRID_W, half), prev_map(COL_UC)),
                  pl.BlockSpec((GRID_W, half), prev_map(COL_UX)),
                  pl.BlockSpec((GRID_W, half), next_map(COL_UC)),
                  pl.BlockSpec((GRID_W, half), next_map(COL_UX)),
                  pl.BlockSpec((3, D_A), lambda i: (0, 0)),
                  pl.BlockSpec((D_A, D_MODEL), lambda i: (0, 0))],
        out_specs=pl.BlockSpec((tm, D_MODEL), lambda i: (i, 0)),
        compiler_params=_cparams(("parallel",)),
        name="short_conv",
    )(u, u, u, u, u, u, u, conv_w, w_out_a_bf)


def _scan_phase1(reverse, units, c):
    ones_bf, bdmask, tri_bf, strict4, incl4, eye4 = c
    n = len(units)
    rng = range(n)
    kk_p = [u[5] for u in units]
    ka_p = [u[6] for u in units]
    rk_p = [u[7] for u in units]

    def bd(x_bf):
        return jnp.concatenate([x_bf] * HEADS_PER_GROUP, axis=0) * ones_bf

    def bd2(x, y):
        return jnp.concatenate([bd(x.astype(BF16)), bd(y.astype(BF16))], axis=1)

    r = [u[0] for u in units]
    k = [u[1] for u in units]
    v = [u[2] for u in units]
    lw = [u[3] for u in units]
    al = [u[4] for u in units]
    kraw = [k[i] * kk_p[i] for i in rng]
    kd = [k[i] * (1.0 + (al[i] - 1.0) * ka_p[i]) for i in rng]
    hs = [_dot(jnp.concatenate(_split(kraw[i] * kraw[i], 2) + _split(r[i] * kd[i] * rk_p[i], 2), axis=0), ones_bf)
          for i in rng]
    kkn = [kraw[i] * lax.rsqrt(hs[i][:CHUNK] + hs[i][CHUNK:2 * CHUNK] + 1e-12) for i in rng]
    b = [kkn[i] * al[i] for i in rng]
    bonus = [(hs[i][2 * CHUNK:3 * CHUNK] + hs[i][3 * CHUNK:]) * v[i] for i in rng]

    parts = [_split(lw[i], 3) for i in rng]
    lc = [sum(_dot(tri_bf, p) for p in parts[i]) for i in rng]
    ltot = [(lc[i][0:1] if reverse else lc[i][CHUNK - 1:CHUNK]) for i in rng]
    decay_col = [jnp.broadcast_to(jnp.exp(ltot[i]), (GROUP, GROUP)).T for i in rng]
    rt = [r[i] * jnp.exp(lc[i]) for i in rng]
    at = [-kkn[i] * jnp.exp(lc[i] - lw[i]) for i in rng]
    einv = [jnp.exp(-lc[i]) for i in rng]
    bh = [b[i] * einv[i] for i in rng]
    kh = [kd[i] * einv[i] for i in rng]
    etail = [jnp.exp(ltot[i] - lc[i]) for i in rng]
    bt = [(b[i] * etail[i]).astype(BF16) for i in rng]
    kt = [(kd[i] * etail[i]).astype(BF16) for i in rng]
    v_bf = [v[i].astype(BF16) for i in rng]

    lhs1 = [jnp.concatenate([at[i], rt[i]], axis=0).astype(BF16) for i in rng]
    ab = [_dot_nt(lhs1[i], bd(bh[i].astype(BF16))) for i in rng]
    ak = [_dot_nt(lhs1[i], bd(kh[i].astype(BF16))) for i in rng]
    a_ab = [jnp.where(strict4, ab[i][:CHUNK], 0.0) for i in rng]
    a_rb = [jnp.where(incl4, ab[i][CHUNK:], 0.0).astype(BF16) for i in rng]
    a_ak = [jnp.where(strict4, ak[i][:CHUNK], 0.0).astype(BF16) for i in rng]
    a_rk = [jnp.where(incl4, ak[i][CHUNK:], 0.0).astype(BF16) for i in rng]

    x = a_ab
    tinv = [eye4 + x[i] for i in rng]
    x = [_dot(x[i].astype(BF16), bd(x[i].astype(BF16))) for i in rng]
    for _ in range(4):
        y = [_dot(jnp.concatenate([x[i], tinv[i]], axis=0).astype(BF16), bd(x[i].astype(BF16))) for i in rng]
        x = [y[i][:CHUNK] for i in rng]
        tinv = [tinv[i] + y[i][CHUNK:] for i in rng]
    tinv = [(tinv[i] + _dot(tinv[i].astype(BF16), bd(x[i].astype(BF16)))).astype(BF16) for i in rng]

    bdv = [bd(v_bf[i]) for i in rng]
    av = [_dot(jnp.concatenate([a_ak[i], a_rk[i]], axis=0), bdv[i]) for i in rng]
    wu = [_dot(tinv[i], bd2(at[i], av[i][:CHUNK])) for i in rng]
    wu_bf = [wu[i].astype(BF16) for i in rng]
    qo = [_dot(a_rb[i], bd2(wu[i][:, :GROUP], wu[i][:, GROUP:])) for i in rng]
    q = [(rt[i] + qo[i][:, :GROUP]).astype(BF16) for i in rng]
    o0 = [qo[i][:, GROUP:] + av[i][CHUNK:] for i in rng]
    zv = [jnp.concatenate([jnp.zeros_like(v_bf[i]), v_bf[i]], axis=1) for i in rng]
    mg = [_dot_tn(jnp.concatenate([bt[i], kt[i]], axis=0), jnp.concatenate([wu_bf[i], zv[i]], axis=0))
          for i in rng]
    qm = [jnp.concatenate([q[i], jnp.where(bdmask, mg[i][:, :GROUP], 0.0).astype(BF16)], axis=0) for i in rng]
    g_bd = [jnp.where(bdmask, mg[i][:, GROUP:], 0.0) for i in rng]
    return [(qm[i], o0[i], g_bd[i], decay_col[i], bonus[i]) for i in rng]


def _scan_body(reverse, has_prev, n_ctx_blk, blk_per_ctx, blk_per_lat, n_blk, *refs):
    if has_prev:
        (r_ref, k_ref, v_ref, wd_ref, ad_ref, w0_ref, wup_ref, a0_ref, aup_ref, s0_ref, kk_ref, ka_ref, rk_ref,
         op_ref, bp_ref, o_ref, bon_ref, sfin_ref, h_ref) = refs
    else:
        (r_ref, k_ref, v_ref, wd_ref, ad_ref, w0_ref, wup_ref, a0_ref, aup_ref, s0_ref, kk_ref, ka_ref, rk_ref,
         o_ref, bon_ref, sfin_ref, h_ref) = refs
        op_ref = bp_ref = None
    s = pl.program_id(1)
    j = (n_blk - 1 - s) if reverse else s
    is_ctx = j < n_ctx_blk
    first_c = (blk_per_ctx - 1) if reverse else 0
    first_l = (blk_per_lat - 1) if reverse else 0
    start = jnp.where(is_ctx, (j % blk_per_ctx) == first_c, ((j - n_ctx_blk) % blk_per_lat) == first_l)

    @pl.when(start)
    def _():
        h_ref[...] = jnp.where(is_ctx, 0.0, s0_ref[...])

    ones_mask = _head_ones()
    ones_bf = ones_mask.astype(F32).astype(BF16)
    t_i = lax.broadcasted_iota(jnp.int32, (CHUNK, CHUNK), 0)
    s_i = lax.broadcasted_iota(jnp.int32, (CHUNK, CHUNK), 1)
    tri = (s_i >= t_i) if reverse else (s_i <= t_i)
    tri_bf = tri.astype(F32).astype(BF16)
    t4 = lax.broadcasted_iota(jnp.int32, (CHUNK, GROUP), 0)
    s4 = lax.broadcasted_iota(jnp.int32, (CHUNK, GROUP), 1) % HEAD_DIM
    incl4 = (s4 >= t4) if reverse else (s4 <= t4)
    strict4 = (s4 > t4) if reverse else (s4 < t4)
    eye4 = (s4 == t4).astype(F32)
    consts = (ones_bf, ones_mask, tri_bf, strict4, incl4, eye4)

    xw = w0_ref[...] + _dot(jnp.tanh(wd_ref[...]).astype(BF16), wup_ref[...])
    lw_all = -math.exp(-0.5) * jax.nn.sigmoid(xw)
    al_all = jax.nn.sigmoid(a0_ref[...] + _dot(ad_ref[...].astype(BF16), aup_ref[...]))

    n_chunks = SCAN_BLOCK // CHUNK
    order = list(range(n_chunks - 1, -1, -1) if reverse else range(n_chunks))
    sls = [slice(cidx * CHUNK, (cidx + 1) * CHUNK) for cidx in order]
    units = []
    for gi in range(SCAN_GROUPS):
        ln = slice(gi * GROUP, (gi + 1) * GROUP)
        units += [(r_ref[sl, ln], k_ref[sl, ln], v_ref[sl, ln], lw_all[sl, ln], al_all[sl, ln],
                   kk_ref[:, ln], ka_ref[:, ln], rk_ref[:, ln]) for sl in sls]
    flat = _scan_phase1(reverse, units, consts)
    pre = [flat[gi * n_chunks:(gi + 1) * n_chunks] for gi in range(SCAN_GROUPS)]

    h_state = [h_ref[gi] for gi in range(SCAN_GROUPS)]
    for ci, sl in enumerate(sls):
        for gi in range(SCAN_GROUPS):
            ln = slice(gi * GROUP, (gi + 1) * GROUP)
            qm, o0, g_bd, decay_col, bonus = pre[gi][ci]
            y = _dot(qm, h_state[gi].astype(BF16))
            o = y[:CHUNK] + o0
            h_state[gi] = decay_col * h_state[gi] + y[CHUNK:] + g_bd
            if has_prev:
                o = o + op_ref[sl, ln]
                bonus = bonus + bp_ref[sl, ln]
            o_ref[sl, ln] = o
            bon_ref[sl, ln] = bonus
    for gi in range(SCAN_GROUPS):
        h_ref[gi] = h_state[gi]
        sfin_ref[gi] = h_state[gi]


def _stage_scan(reverse, u, w0, w_up_bf, a0, a_up_bf, s0_bd, k_k, k_a, r_k, prev,
                n_ctx_seq, ctx_len, n_lat_seq, lat_len):
    t = u.shape[0]
    blk_per_ctx = ctx_len // SCAN_BLOCK
    blk_per_lat = lat_len // SCAN_BLOCK
    n_ctx_blk = n_ctx_seq * blk_per_ctx
    n_blk = t // SCAN_BLOCK
    has_prev = prev is not None

    def jmap(s):
        return (n_blk - 1 - s) if reverse else s

    width = SCAN_GROUPS * GROUP

    def tok(colbase):
        return pl.BlockSpec((SCAN_BLOCK, width), lambda g, s: (jmap(s), colbase // width + g))

    def s0_map(g, s):
        j = jmap(s)
        b = jnp.clip((j - n_ctx_blk) // blk_per_lat, 0, n_lat_seq - 1)
        return (b, g, 0, 0)

    def sfin_map(g, s):
        j = jmap(s)
        b = jnp.where(j < n_ctx_blk, j // blk_per_ctx, n_ctx_seq)
        return (b, g, 0, 0)

    d = 1 if reverse else 0
    lora = lambda slot: pl.BlockSpec((SCAN_BLOCK, LORA_PAD), lambda g, s: (jmap(s), COL_LORA // LORA_PAD + slot))
    par = pl.BlockSpec((1, width), lambda g, s: (0, g))
    dpar = pl.BlockSpec((None, 1, width), lambda g, s: (d, 0, g))
    dup = pl.BlockSpec((None, LORA_PAD, width), lambda g, s: (d, 0, g))
    state_spec = lambda imap: pl.BlockSpec((None, SCAN_GROUPS, GROUP, GROUP), imap)
    in_specs = [tok(COL_R), tok(COL_K), tok(COL_V), lora(d), lora(2 + d), dpar, dup, dpar, dup,
                state_spec(s0_map), par, par, par]
    args = [u, u, u, u, u, w0[:, None, :], w_up_bf, a0[:, None, :], a_up_bf, s0_bd, k_k, k_a, r_k]
    if has_prev:
        in_specs += [tok(0), tok(0)]
        args += list(prev)
    out = jax.ShapeDtypeStruct((t, D_B), F32)
    body = functools.partial(_scan_body, reverse, has_prev, n_ctx_blk, blk_per_ctx, blk_per_lat, n_blk)
    return pl.pallas_call(
        body,
        out_shape=(out, out, jax.ShapeDtypeStruct((n_ctx_seq + 1, N_GROUPS, GROUP, GROUP), F32)),
        grid=(N_GROUPS // SCAN_GROUPS, n_blk),
        in_specs=in_specs,
        out_specs=(tok(0), tok(0), state_spec(sfin_map)),
        scratch_shapes=[pltpu.VMEM((SCAN_GROUPS, GROUP, GROUP), F32)],
        compiler_params=_cparams(("parallel", "arbitrary")),
        name="rwkv_scan_bwd" if reverse else "rwkv_scan_fwd",
    )(*args)


def _post_body(o_ref, bon_ref, lora_ref, lng_ref, lnb_ref, gup_ref, wo_ref, y_ref):
    ones_bf = _head_ones().astype(F32).astype(BF16)
    o = o_ref[...]
    mu = _headsum(o, ones_bf) * (1.0 / HEAD_DIM)
    d = o - mu
    var = _headsum(d * d, ones_bf) * (1.0 / HEAD_DIM)
    on = d * lax.rsqrt(var + GN_EPS) * lng_ref[...] + lnb_ref[...] + bon_ref[...]
    gd = lora_ref[:, 4 * LORA_PAD:4 * LORA_PAD + LORA_G]
    g = _dot(jax.nn.sigmoid(gd).astype(BF16), gup_ref[...])
    y_ref[...] = _dot((on * g).astype(BF16), wo_ref[...])


def _stage_post(o, bon, u, ln_g, ln_b, g_up_bf, w_o_bf):
    t = o.shape[0]
    tm = 256
    return pl.pallas_call(
        _post_body,
        out_shape=jax.ShapeDtypeStruct((t, D_MODEL), F32),
        grid=(t // tm,),
        in_specs=[pl.BlockSpec((tm, D_B), lambda i: (i, 0)),
                  pl.BlockSpec((tm, D_B), lambda i: (i, 0)),
                  pl.BlockSpec((tm, LORA_BLOCK), lambda i: (i, COL_LORA // LORA_BLOCK)),
                  pl.BlockSpec((1, D_B), lambda i: (0, 0)),
                  pl.BlockSpec((1, D_B), lambda i: (0, 0)),
                  pl.BlockSpec((LORA_G, D_B), lambda i: (0, 0)),
                  pl.BlockSpec((D_B, D_MODEL), lambda i: (0, 0))],
        out_specs=pl.BlockSpec((tm, D_MODEL), lambda i: (i, 0)),
        compiler_params=_cparams(("parallel",)),
        name="rwkv_post",
    )(o, bon, u, ln_g, ln_b, g_up_bf, w_o_bf)


def _merge_body(ya_ref, yb_ref, ga_ref, gb_ref, x_ref, gt_ref, sh_ref, sc_ref, g_ref, w_ref, x1_ref, h2_ref):
    m = jax.nn.sigmoid(ga_ref[...]) * ya_ref[...] + jax.nn.sigmoid(gb_ref[...]) * yb_ref[...]
    x1 = x_ref[...] + gt_ref[...] * _dot(m.astype(BF16), w_ref[...])
    x1_ref[...] = x1
    h2_ref[...] = _rms(x1) * g_ref[...] * (1.0 + sc_ref[...]) + sh_ref[...]


def _stage_merge(y_a, y_b, u, x_all, mod4, norm2_g, w_mix_bf, n_ctx_tok, lat_len):
    t = x_all.shape[0]
    tm = 256
    full = lambda col: pl.BlockSpec((tm, D_MODEL), lambda i: (i, col // D_MODEL))
    out = jax.ShapeDtypeStruct((t, D_MODEL), F32)
    return pl.pallas_call(
        _merge_body,
        out_shape=(out, out),
        grid=(t // tm,),
        in_specs=[full(0), full(0), full(COL_GA), full(COL_GB), full(0),
                  _mod_spec(2, tm, n_ctx_tok, lat_len),
                  _mod_spec(3, tm, n_ctx_tok, lat_len),
                  _mod_spec(4, tm, n_ctx_tok, lat_len),
                  pl.BlockSpec((1, D_MODEL), lambda i: (0, 0)),
                  pl.BlockSpec((D_MODEL, D_MODEL), lambda i: (0, 0))],
        out_specs=(full(0), full(0)),
        compiler_params=_cparams(("parallel",)),
        name="merge_norm2",
    )(y_a, y_b, u, u, x_all, mod4, mod4, mod4, norm2_g, w_mix_bf)


TOPK_TOKENS = 256


def _extract_topk(s, pos, limit, payload=None):
    vals, outs = [], []
    for _ in range(PEER_TOPK):
        m = jnp.max(s, axis=0, keepdims=True)
        pm = jnp.min(jnp.where(s == m, pos, limit), axis=0, keepdims=True)
        sel = pos == pm
        vals.append(m)
        if payload is None:
            outs.append(pm)
        else:
            outs.append(jnp.max(jnp.where(sel, payload, -1), axis=0, keepdims=True))
        s = jnp.where(sel, -jnp.inf, s)
    return jnp.concatenate(vals, axis=0), jnp.concatenate(outs, axis=0)


def _sort_network(n):
    def merge(lo, hi, r):
        step = r * 2
        if step < hi - lo:
            yield from merge(lo, hi, step)
            yield from merge(lo + r, hi, step)
            yield from [(i, i + r) for i in range(lo + r, hi - r, step)]
        else:
            yield (lo, lo + r)

    def sort(lo, hi):
        if hi - lo >= 1:
            mid = lo + (hi - lo) // 2
            yield from sort(lo, mid)
            yield from sort(mid + 1, hi)
            yield from merge(lo, hi, 1)

    return list(sort(0, n - 1))


SUBLANES = 8


def _topk_keys(s):
    n = N_KEYS // SUBLANES
    assert n == PEER_TOPK
    val = [s[SUBLANES * g:SUBLANES * (g + 1)] for g in range(n)]
    row = lax.broadcasted_iota(jnp.int32, (SUBLANES, s.shape[1]), 0)
    idx = [row + SUBLANES * g for g in range(n)]
    for a, b in _sort_network(n):
        first = (val[a] > val[b]) | ((val[a] == val[b]) & (idx[a] < idx[b]))
        val[a], val[b] = jnp.where(first, val[a], val[b]), jnp.where(first, val[b], val[a])
        idx[a], idx[b] = jnp.where(first, idx[a], idx[b]), jnp.where(first, idx[b], idx[a])
    vals, outs = [], []
    for rnd in range(PEER_TOPK):
        m = jnp.max(val[0], axis=0, keepdims=True)
        pm = jnp.min(jnp.where(val[0] == m, idx[0], N_KEYS), axis=0, keepdims=True)
        vals.append(m)
        outs.append(pm)
        win = idx[0] == pm
        for g in range(PEER_TOPK - 1 - rnd):
            val[g] = jnp.where(win, val[g + 1], val[g])
            idx[g] = jnp.where(win, idx[g + 1], idx[g])
    return jnp.concatenate(vals, axis=0), jnp.concatenate(outs, axis=0)


def _topk_body(h_ref, wq_ref, keys_ref, eid_ref, gate_ref, q_ref, eid_s, gate_s):
    q_ref[...] = _dot_nt(wq_ref[...], h_ref[...].astype(BF16))
    half = PEER_TOPK // 2
    iota = lambda rows: lax.broadcasted_iota(jnp.int32, (rows, TOPK_TOKENS), 0)
    cpos = jnp.concatenate([iota(PEER_TOPK)] + [iota(half) + PEER_TOPK * i for i in range(1, half)]
                           + [(iota(half) + half) * PEER_TOPK], axis=0)

    def head(hd, carry):
        tops = []
        for p in range(2):
            off = pl.multiple_of(hd * (2 * HALF_KEY) + p * HALF_KEY, HALF_KEY)
            qs = q_ref[pl.ds(off, HALF_KEY), :].astype(BF16)
            s = _dot(keys_ref[p * PEER_HEADS + hd], qs)
            tops.append(_topk_keys(s))
        (s1, i1), (s2, i2) = tops
        cand = jnp.concatenate([s1[0:1] + s2] + [s1[i:i + 1] + s2[:half] for i in range(1, half)]
                               + [s1[half:] + s2[0:1]], axis=0)
        ids = jnp.concatenate([i1[0:1] * N_KEYS + i2] + [i1[i:i + 1] * N_KEYS + i2[:half] for i in range(1, half)]
                              + [i1[half:] * N_KEYS + i2[0:1]], axis=0)
        tv, te = _extract_topk(cand, cpos, PEER_TOPK * PEER_TOPK, payload=ids)
        ex = jnp.exp(tv - tv[0:1])
        gate = ex / jnp.sum(ex, axis=0, keepdims=True)
        row = pl.multiple_of(hd * PEER_TOPK, PEER_TOPK)
        eid_s[pl.ds(row, PEER_TOPK), :] = te
        gate_s[pl.ds(row, PEER_TOPK), :] = gate
        return carry

    lax.fori_loop(0, PEER_HEADS, head, 0)
    eid_ref[...] = eid_s[...].T
    gate_ref[...] = gate_s[...].T


def _stage_topk(h2, w_q_t_bf, keys_bf):
    t = h2.shape[0]
    tb = TOPK_TOKENS
    return pl.pallas_call(
        _topk_body,
        out_shape=(jax.ShapeDtypeStruct((t, EXPERTS_PER_TOKEN), jnp.int32),
                   jax.ShapeDtypeStruct((t, EXPERTS_PER_TOKEN), F32)),
        grid=(t // tb,),
        in_specs=[pl.BlockSpec((tb, D_MODEL), lambda i: (i, 0)),
                  pl.BlockSpec((PEER_HEADS * 2 * HALF_KEY, D_MODEL), lambda i: (0, 0)),
                  pl.BlockSpec((2 * PEER_HEADS, N_KEYS, HALF_KEY), lambda i: (0, 0, 0))],
        out_specs=(pl.BlockSpec((tb, EXPERTS_PER_TOKEN), lambda i: (i, 0)),
                   pl.BlockSpec((tb, EXPERTS_PER_TOKEN), lambda i: (i, 0))),
        scratch_shapes=[pltpu.VMEM((PEER_HEADS * 2 * HALF_KEY, tb), F32),
                        pltpu.VMEM((EXPERTS_PER_TOKEN, tb), jnp.int32),
                        pltpu.VMEM((EXPERTS_PER_TOKEN, tb), F32)],
        compiler_params=_cparams(("parallel",)),
        name="peer_topk",
    )(h2, w_q_t_bf, keys_bf)


def _gather_copy(tab_ref, buf, sem, row, expert):
    return pltpu.make_async_copy(tab_ref.at[expert], buf.at[pl.ds(row, 1), :], sem)


def _gather_issue(tab_ref, id_ref, row0, buf, sem):
    for t in range(GATHER_TOKENS):
        for e in range(EXPERTS_PER_TOKEN):
            _gather_copy(tab_ref, buf, sem, t * EXPERTS_PER_TOKEN + e, id_ref[row0 + t, e]).start(priority=e % 2)


def _gather_wait(tab_ref, buf, sem):
    for r in range(GATHER_TOKENS * EXPERTS_PER_TOKEN):
        _gather_copy(tab_ref, buf, sem, r, 0).wait()


def _mix_tokens(buf, gate, h, eye):
    ne = EXPERTS_PER_TOKEN
    n_tiles = D_MODEL // LANE
    outs = []
    for t in range(GATHER_TOKENS):
        rows = slice(t * ne, (t + 1) * ne)
        acc = jnp.zeros((ne, LANE), F32)
        for j in range(n_tiles):
            w = buf[rows, j * LANE:(j + 1) * LANE]
            u = lax.bitcast_convert_type(w << 16, F32)
            acc = acc + u * h[t:t + 1, j * LANE:(j + 1) * LANE]
        pre = jnp.sum(acc, axis=-1, keepdims=True)
        gcol = jnp.sum(jnp.where(eye, gate[t:t + 1, :], 0.0), axis=-1, keepdims=True)
        coef = gcol * jax.nn.gelu(pre, approximate=True)
        cb = jnp.broadcast_to(coef, (ne, LANE))
        tiles = []
        for j in range(n_tiles):
            w = buf[rows, j * LANE:(j + 1) * LANE]
            v = lax.bitcast_convert_type(w & jnp.uint32(0xFFFF0000), F32)
            tiles.append(jnp.sum(cb * v, axis=0, keepdims=True))
        outs.append(jnp.concatenate(tiles, axis=1))
    return jnp.concatenate(outs, axis=0)


def _gather_body(ids_ref, idn_ref, gate_ref, h_ref, x_ref, gt_ref, fg_ref, tab_ref, o_ref,
                 buf0, buf1, sem):
    i = pl.program_id(0)
    n = pl.num_programs(0)
    tg = GATHER_TOKENS
    eye = (lax.broadcasted_iota(jnp.int32, (EXPERTS_PER_TOKEN, EXPERTS_PER_TOKEN), 0)
           == lax.broadcasted_iota(jnp.int32, (EXPERTS_PER_TOKEN, EXPERTS_PER_TOKEN), 1))

    @pl.when(i == 0)
    def _():
        _gather_issue(tab_ref, ids_ref, 0, buf0, sem.at[0])

    def finish(mix, lo):
        x2 = x_ref[lo:lo + tg, :] + gt_ref[...] * mix
        o_ref[lo:lo + tg, :] = _rms(x2) * fg_ref[...]

    _gather_wait(tab_ref, buf0, sem.at[0])
    _gather_issue(tab_ref, ids_ref, tg, buf1, sem.at[1])
    finish(_mix_tokens(buf0, gate_ref[0:tg, :], h_ref[0:tg, :], eye), 0)
    _gather_wait(tab_ref, buf1, sem.at[1])
    _gather_issue(tab_ref, idn_ref, 0, buf0, sem.at[0])
    finish(_mix_tokens(buf1, gate_ref[tg:2 * tg, :], h_ref[tg:2 * tg, :], eye), tg)

    @pl.when(i == n - 1)
    def _():
        _gather_wait(tab_ref, buf0, sem.at[0])


def _stage_gather(eid, gate, h2, x1, mod4, final_g, tab, n_ctx_tok, lat_len):
    t = h2.shape[0]
    tg = GATHER_TOKENS
    ts = 2 * tg
    n = t // ts
    tokspec = lambda w: pl.BlockSpec((ts, w), lambda i: (i, 0))
    rows = tg * EXPERTS_PER_TOKEN
    return pl.pallas_call(
        _gather_body,
        out_shape=jax.ShapeDtypeStruct((t, D_MODEL), F32),
        grid=(n,),
        in_specs=[pl.BlockSpec((ts, EXPERTS_PER_TOKEN), lambda i: (i, 0), memory_space=pltpu.SMEM),
                  pl.BlockSpec((tg, EXPERTS_PER_TOKEN), lambda i: (jnp.minimum(2 * i + 2, 2 * n - 1), 0),
                               memory_space=pltpu.SMEM),
                  tokspec(EXPERTS_PER_TOKEN), tokspec(D_MODEL), tokspec(D_MODEL),
                  _mod_spec(5, ts, n_ctx_tok, lat_len),
                  pl.BlockSpec((1, D_MODEL), lambda i: (0, 0)),
                  pl.BlockSpec(memory_space=pl.ANY)],
        out_specs=tokspec(D_MODEL),
        scratch_shapes=[pltpu.VMEM((rows, D_MODEL), jnp.uint32),
                        pltpu.VMEM((rows, D_MODEL), jnp.uint32),
                        pltpu.SemaphoreType.DMA((2,))],
        compiler_params=_cparams(("arbitrary",)),
        name="peer_gather",
    )(eid, eid, gate, h2, x1, mod4, final_g, tab)


def _pack_body(u_ref, v_ref, o_ref):
    bits = lambda x: lax.bitcast_convert_type(x.astype(BF16).astype(F32), jnp.uint32)
    o_ref[...] = (bits(u_ref[...]) >> 16) | (bits(v_ref[...]) & jnp.uint32(0xFFFF0000))


def _pack_expert_tables(u_tab, v_tab):
    n, d = u_tab.shape
    tn = 512
    spec = pl.BlockSpec((tn, d), lambda i: (i, 0))
    packed = pl.pallas_call(
        _pack_body,
        out_shape=jax.ShapeDtypeStruct((n, d), jnp.uint32),
        grid=(n // tn,),
        in_specs=[spec, spec],
        out_specs=spec,
        compiler_params=_cparams(("parallel",)),
        name="pack_expert_tables",
    )(u_tab, v_tab)
    return packed[:, None, :]


def _pad_rows(w, rows):
    return jnp.pad(w, ((0, 0), (0, rows - w.shape[1]), (0, 0)))


def _layer_weights(w_in, w_up, a_up):
    widths = (D_A, D_A, D_A, D_B, D_B, D_B, LORA_W, LORA_W, LORA_A, LORA_A, LORA_G, D_MODEL, D_MODEL)
    cuts = [0]
    for w in widths:
        cuts.append(cuts[-1] + w)
    seg = [w_in[:, cuts[i]:cuts[i + 1]] for i in range(len(widths))]
    (u_b, u_c, u_x, r, k, v, wd_f, wd_b, ad_f, ad_b, gd, gate_a, gate_b) = seg
    padc = lambda m: jnp.pad(m, ((0, 0), (0, LORA_PAD - m.shape[1])))
    tail = jnp.zeros((w_in.shape[0], LORA_BLOCK - 4 * LORA_PAD - LORA_G), w_in.dtype)
    w_cat = jnp.concatenate([r, k, v, gate_a, gate_b, u_b, u_c, u_x,
                             padc(wd_f), padc(wd_b), padc(ad_f), padc(ad_b), gd, tail], axis=1)
    return w_cat.astype(BF16), _pad_rows(w_up, LORA_PAD).astype(BF16), _pad_rows(a_up, LORA_PAD).astype(BF16)


def _state_to_blockdiag(s0):
    b = s0.shape[0]
    st = jnp.swapaxes(s0, -1, -2).reshape(b, N_GROUPS, HEADS_PER_GROUP, HEAD_DIM, HEAD_DIM)
    eye = jnp.eye(HEADS_PER_GROUP, dtype=s0.dtype)
    bd = jnp.einsum('bgikv,ij->bgikjv', st, eye)
    return bd.reshape(b, N_GROUPS, GROUP, GROUP)


def _blockdiag_to_state(bd):
    b = bd.shape[0]
    x = bd.reshape(b, N_GROUPS, HEADS_PER_GROUP, HEAD_DIM, HEADS_PER_GROUP, HEAD_DIM)
    d = jnp.diagonal(x, axis1=2, axis2=4)
    d = jnp.moveaxis(d, -1, 2)
    return jnp.swapaxes(d, -1, -2).reshape(b, H_B, HEAD_DIM, HEAD_DIM)


def kernel(x_prompt, x_sample, state_rwkv, c, c_ctx, w_ada, b_ada, norm1_g, norm2_g, w_in, conv_w, w_out_a, w0, w_up, a0, a_up, g_up, k_k, k_a, r_k, ln_g, ln_b, w_o, w_mix, w_q, sub_keys, u_tab, v_tab, final_g):
    n_ctx_seq, ctx_len, _ = x_prompt.shape
    n_lat_seq, lat_len, _ = x_sample.shape
    depth = w_in.shape[0]
    n_ctx_tok = n_ctx_seq * ctx_len
    assert depth == 1, "the final norm is fused into the last stage of a single layer"
    assert ctx_len % SCAN_BLOCK == 0 and lat_len % 1024 == 0 and n_ctx_tok % 1024 == 0
    assert lat_len % (GRID_W * (CONV_TILE // GRID_W)) == 0

    x_all = jnp.concatenate([x_prompt.reshape(n_ctx_tok, D_MODEL),
                             x_sample.reshape(n_lat_seq * lat_len, D_MODEL)], axis=0)
    mod_rows = 16
    cc = jnp.zeros((mod_rows, D_MODEL), F32).at[0].set(c_ctx).at[1:1 + n_lat_seq].set(c)
    new_states = []
    for l in range(depth):
        w_cat, w_up_bf, a_up_bf = _layer_weights(w_in[l], w_up[l], a_up[l])
        mod = _stage_mod(cc, w_ada[l].astype(BF16), b_ada[l][None, :])
        mod4 = mod.reshape(mod_rows, 6, 1, D_MODEL)
        u = _stage_inproj(x_all, mod4, norm1_g[l][None, :], w_cat, n_ctx_tok, lat_len)
        y_a = _stage_conv(u, conv_w[l], w_out_a[l].astype(BF16), n_ctx_tok, lat_len)
        kk2, ka2, rk2 = k_k[l][None, :], k_a[l][None, :], r_k[l].reshape(1, D_B)
        seq = (n_ctx_seq, ctx_len, n_lat_seq, lat_len)
        lowrank = (w0[l], w_up_bf, a0[l], a_up_bf)
        o_f, bon_f, sfin_f = _stage_scan(False, u, *lowrank, _state_to_blockdiag(state_rwkv[:, l, 0]),
                                         kk2, ka2, rk2, None, *seq)
        o, bon, sfin_b = _stage_scan(True, u, *lowrank, _state_to_blockdiag(state_rwkv[:, l, 1]),
                                     kk2, ka2, rk2, (o_f, bon_f), *seq)
        new_states.append(jnp.stack([_blockdiag_to_state(sfin_f[:n_ctx_seq]),
                                     _blockdiag_to_state(sfin_b[:n_ctx_seq])], axis=1))
        y_b = _stage_post(o, bon, u, ln_g[l][None, :], ln_b[l][None, :], g_up[l].astype(BF16), w_o[l].astype(BF16))
        x1, h2 = _stage_merge(y_a, y_b, u, x_all, mod4, norm2_g[l][None, :], w_mix[l].astype(BF16),
                              n_ctx_tok, lat_len)
        keys = sub_keys[l].reshape(2 * PEER_HEADS, N_KEYS, HALF_KEY).astype(BF16)
        eid, gate = _stage_topk(h2, w_q[l].T.astype(BF16), keys)
        tab = _pack_expert_tables(u_tab[l], v_tab[l])
        x_all = _stage_gather(eid, gate, h2, x1, mod4, final_g[None, :], tab, n_ctx_tok, lat_len)
    y_prompt = x_all[:n_ctx_tok].reshape(x_prompt.shape)
    y_sample = x_all[n_ctx_tok:].reshape(x_sample.shape)
    return (y_prompt, y_sample, jnp.stack(new_states, axis=1))
```

```python
import functools
import math

import jax
import jax.numpy as jnp
from jax import lax
from jax.experimental import pallas as pl
from jax.experimental.pallas import tpu as pltpu

F32 = jnp.float32
BF16 = jnp.bfloat16

D_MODEL = 2048
GRID_W = 64
D_A = 1024
H_B = 32
HEAD_DIM = 64
D_B = H_B * HEAD_DIM
LORA_W = 96
LORA_A = 96
LORA_G = 256
PEER_HEADS = 8
N_KEYS = 128
HALF_KEY = 128
PEER_TOPK = 16
RMS_EPS = 1e-6
GN_EPS = 64e-5

LANE = 128
CHUNK = 64
SCAN_BLOCK = 256
GROUP = 128
HEADS_PER_GROUP = GROUP // HEAD_DIM
N_GROUPS = D_B // GROUP
SCAN_GROUPS = 8
LORA_PAD = 128
EXPERTS_PER_TOKEN = PEER_HEADS * PEER_TOPK
GATHER_TOKENS = 8
GATHER_SLOTS = 4
GATHER_AHEAD = 3
VMEM_LIMIT = 56 * 1024 * 1024

COL_R, COL_K, COL_V, COL_GA, COL_GB = 0, 2048, 4096, 6144, 8192
COL_UB, COL_UC, COL_UX, COL_LORA = 10240, 11264, 12288, 13312
D_IN_PAD = 14336
LORA_BLOCK = 1024


def _cparams(sem, vmem=VMEM_LIMIT):
    return pltpu.CompilerParams(dimension_semantics=sem, vmem_limit_bytes=vmem)


def _dot(a, b):
    return jnp.dot(a, b, preferred_element_type=F32)


def _dot_nt(a, b):
    return lax.dot_general(a, b, (((1,), (1,)), ((), ())), preferred_element_type=F32)


def _dot_tn(a, b):
    return lax.dot_general(a, b, (((0,), (0,)), ((), ())), preferred_element_type=F32)


def _split(x, parts):
    out = []
    for _ in range(parts):
        p = x.astype(BF16)
        out.append(p)
        x = x - p.astype(F32)
    return out


def _head_ones():
    r = lax.broadcasted_iota(jnp.int32, (GROUP, GROUP), 0) // HEAD_DIM
    c = lax.broadcasted_iota(jnp.int32, (GROUP, GROUP), 1) // HEAD_DIM
    return r == c


def _headsum(x, ones_bf):
    outs = []
    for g in range(x.shape[1] // GROUP):
        xs = x[:, g * GROUP:(g + 1) * GROUP]
        hi, lo = _split(xs, 2)
        outs.append(_dot(hi, ones_bf) + _dot(lo, ones_bf))
    return outs[0] if len(outs) == 1 else jnp.concatenate(outs, axis=1)


def _rms(x):
    return x * lax.rsqrt(jnp.mean(x * x, axis=-1, keepdims=True) + RMS_EPS)


def _mod_body(c_ref, w_ref, b_ref, o_ref):
    c = c_ref[...]
    s = (c * jax.nn.sigmoid(c)).astype(BF16)
    o_ref[...] = _dot(s, w_ref[...]) + b_ref[...]


def _stage_mod(cc, w_ada_bf, b_ada):
    rows, d = cc.shape
    n = w_ada_bf.shape[1]
    tn = 2048
    return pl.pallas_call(
        _mod_body,
        out_shape=jax.ShapeDtypeStruct((rows, n), F32),
        grid=(n // tn,),
        in_specs=[pl.BlockSpec((rows, d), lambda j: (0, 0)),
                  pl.BlockSpec((d, tn), lambda j: (0, j)),
                  pl.BlockSpec((1, tn), lambda j: (0, j))],
        out_specs=pl.BlockSpec((rows, tn), lambda j: (0, j)),
        compiler_params=_cparams(("arbitrary",)),
        name="adaln_mod",
    )(cc, w_ada_bf, b_ada)


def _mod_spec(which, tm, n_ctx_tok, lat_len):
    nct = n_ctx_tok // tm
    per = lat_len // tm

    def imap(i, *_):
        row = jnp.where(i < nct, 0, 1 + (i - nct) // per)
        return (row, which, 0, 0)

    return pl.BlockSpec((None, None, 1, D_MODEL), imap)


def _inproj_body(x_ref, sh_ref, sc_ref, g_ref, w_ref, o_ref, h_ref):
    @pl.when(pl.program_id(1) == 0)
    def _():
        h = _rms(x_ref[...]) * g_ref[...] * (1.0 + sc_ref[...]) + sh_ref[...]
        h_ref[...] = h.astype(BF16)

    o_ref[...] = _dot(h_ref[...], w_ref[...])


def _stage_inproj(x_all, mod4, norm1_g, w_cat, n_ctx_tok, lat_len):
    t = x_all.shape[0]
    tm, tn = 1024, 1024
    return pl.pallas_call(
        _inproj_body,
        out_shape=jax.ShapeDtypeStruct((t, D_IN_PAD), F32),
        grid=(t // tm, D_IN_PAD // tn),
        in_specs=[pl.BlockSpec((tm, D_MODEL), lambda i, j: (i, 0)),
                  _mod_spec(0, tm, n_ctx_tok, lat_len),
                  _mod_spec(1, tm, n_ctx_tok, lat_len),
                  pl.BlockSpec((1, D_MODEL), lambda i, j: (0, 0)),
                  pl.BlockSpec((D_MODEL, tn), lambda i, j: (0, j))],
        out_specs=pl.BlockSpec((tm, tn), lambda i, j: (i, j)),
        scratch_shapes=[pltpu.VMEM((tm, D_MODEL), BF16)],
        compiler_params=_cparams(("parallel", "arbitrary")),
        name="norm1_inproj",
    )(x_all, mod4, mod4, norm1_g, w_cat)


CONV_TILE = 256


def _conv_body(n_ctx_tiles, tiles_per_lat, ub_ref, uc_ref, ux_ref, pc_ref, px_ref, nc_ref, nx_ref,
               cw_ref, wo_ref, y_ref):
    i = pl.program_id(0)
    is_lat = i >= n_ctx_tiles
    jl = (i - n_ctx_tiles) % tiles_per_lat
    half = D_A // 2
    z = uc_ref[...] * ux_ref[...]
    cw = cw_ref[...]
    row = lax.broadcasted_iota(jnp.int32, (CONV_TILE, 1), 0)
    pmask = jnp.where(is_lat, GRID_W - 1, CONV_TILE - 1)
    keep_p = (row & pmask) != 0
    keep_n = (row & pmask) != pmask
    zp = jnp.where(keep_p, pltpu.roll(z, 1, 0), 0.0)
    zn = jnp.where(keep_n, pltpu.roll(z, CONV_TILE - 1, 0), 0.0)
    conv_h = cw[0:1] * zp + cw[1:2] * z + cw[2:3] * zn
    zv = z[:, half:]
    hp = jnp.where(jl != 0, pc_ref[...] * px_ref[...], 0.0)
    hn = jnp.where(jl != tiles_per_lat - 1, nc_ref[...] * nx_ref[...], 0.0)
    vp = jnp.concatenate([hp, zv[:CONV_TILE - GRID_W]], axis=0)
    vn = jnp.concatenate([zv[GRID_W:], hn], axis=0)
    conv_v = cw[0:1, half:] * vp + cw[1:2, half:] * zv + cw[2:3, half:] * vn
    zc = jnp.concatenate([conv_h[:, :half], jnp.where(is_lat, conv_v, conv_h[:, half:])], axis=1)
    y_ref[...] = _dot((ub_ref[...] * zc).astype(BF16), wo_ref[...])


def _stage_conv(u, conv_w, w_out_a_bf, n_ctx_tok, lat_len):
    t = u.shape[0]
    tm = CONV_TILE
    n_ctx_tiles = n_ctx_tok // tm
    tiles_per_lat = lat_len // tm
    rpt = tm // GRID_W
    last_halo = t // GRID_W - 1
    half = D_A // 2
    cb = lambda col: col // D_A
    hb = lambda col: (col + half) // half
    body = functools.partial(_conv_body, n_ctx_tiles, tiles_per_lat)
    prev_map = lambda col: (lambda i: (jnp.maximum(i * rpt - 1, 0), hb(col)))
    next_map = lambda col: (lambda i: (jnp.minimum(i * rpt + rpt, last_halo), hb(col)))
    return pl.pallas_call(
        body,
        out_shape=jax.ShapeDtypeStruct((t, D_MODEL), F32),
        grid=(t // tm,),
        in_specs=[pl.BlockSpec((tm, D_A), lambda i: (i, cb(COL_UB))),
                  pl.BlockSpec((tm, D_A), lambda i: (i, cb(COL_UC))),
                  pl.BlockSpec((tm, D_A), lambda i: (i, cb(COL_UX))),
                  pl.BlockSpec((GRID_W, half), prev_map(COL_UC)),
                  pl.BlockSpec((GRID_W, half), prev_map(COL_UX)),
                  pl.BlockSpec((GRID_W, half), next_map(COL_UC)),
                  pl.BlockSpec((GRID_W, half), next_map(COL_UX)),
                  pl.BlockSpec((3, D_A), lambda i: (0, 0)),
                  pl.BlockSpec((D_A, D_MODEL), lambda i: (0, 0))],
        out_specs=pl.BlockSpec((tm, D_MODEL), lambda i: (i, 0)),
        compiler_params=_cparams(("parallel",)),
        name="short_conv",
    )(u, u, u, u, u, u, u, conv_w, w_out_a_bf)


def _scan_phase1(reverse, units, c):
    ones_bf, bdmask, tri_bf, strict4, incl4, eye4 = c
    n = len(units)
    rng = range(n)
    kk_p = [u[5] for u in units]
    ka_p = [u[6] for u in units]
    rk_p = [u[7] for u in units]

    def bd(x_bf):
        return jnp.concatenate([x_bf] * HEADS_PER_GROUP, axis=0) * ones_bf

    def bd2(x, y):
        return jnp.concatenate([bd(x.astype(BF16)), bd(y.astype(BF16))], axis=1)

    r = [u[0] for u in units]
    k = [u[1] for u in units]
    v = [u[2] for u in units]
    lw = [u[3] for u in units]
    al = [u[4] for u in units]
    kraw = [k[i] * kk_p[i] for i in rng]
    kd = [k[i] * (1.0 + (al[i] - 1.0) * ka_p[i]) for i in rng]
    hs = [_dot(jnp.concatenate(_split(kraw[i] * kraw[i], 2) + _split(r[i] * kd[i] * rk_p[i], 2), axis=0), ones_bf)
          for i in rng]
    kkn = [kraw[i] * lax.rsqrt(hs[i][:CHUNK] + hs[i][CHUNK:2 * CHUNK] + 1e-12) for i in rng]
    b = [kkn[i] * al[i] for i in rng]
    bonus = [(hs[i][2 * CHUNK:3 * CHUNK] + hs[i][3 * CHUNK:]) * v[i] for i in rng]

    parts = [_split(lw[i], 3) for i in rng]
    lc = [sum(_dot(tri_bf, p) for p in parts[i]) for i in rng]
    ltot = [(lc[i][0:1] if reverse else lc[i][CHUNK - 1:CHUNK]) for i in rng]
    decay_col = [jnp.broadcast_to(jnp.exp(ltot[i]), (GROUP, GROUP)).T for i in rng]
    rt = [r[i] * jnp.exp(lc[i]) for i in rng]
    at = [-kkn[i] * jnp.exp(lc[i] - lw[i]) for i in rng]
    einv = [jnp.exp(-lc[i]) for i in rng]
    bh = [b[i] * einv[i] for i in rng]
    kh = [kd[i] * einv[i] for i in rng]
    etail = [jnp.exp(ltot[i] - lc[i]) for i in rng]
    bt = [(b[i] * etail[i]).astype(BF16) for i in rng]
    kt = [(kd[i] * etail[i]).astype(BF16) for i in rng]
    v_bf = [v[i].astype(BF16) for i in rng]

    lhs1 = [jnp.concatenate([at[i], rt[i]], axis=0).astype(BF16) for i in rng]
    ab = [_dot_nt(lhs1[i], bd(bh[i].astype(BF16))) for i in rng]
    ak = [_dot_nt(lhs1[i], bd(kh[i].astype(BF16))) for i in rng]
    a_ab = [jnp.where(strict4, ab[i][:CHUNK], 0.0) for i in rng]
    a_rb = [jnp.where(incl4, ab[i][CHUNK:], 0.0).astype(BF16) for i in rng]
    a_ak = [jnp.where(strict4, ak[i][:CHUNK], 0.0).astype(BF16) for i in rng]
    a_rk = [jnp.where(incl4, ak[i][CHUNK:], 0.0).astype(BF16) for i in rng]

    x = a_ab
    tinv = [eye4 + x[i] for i in rng]
    x = [_dot(x[i].astype(BF16), bd(x[i].astype(BF16))) for i in rng]
    for _ in range(4):
        y = [_dot(jnp.concatenate([x[i], tinv[i]], axis=0).astype(BF16), bd(x[i].astype(BF16))) for i in rng]
        x = [y[i][:CHUNK] for i in rng]
        tinv = [tinv[i] + y[i][CHUNK:] for i in rng]
    tinv = [(tinv[i] + _dot(tinv[i].astype(BF16), bd(x[i].astype(BF16)))).astype(BF16) for i in rng]

    bdv = [bd(v_bf[i]) for i in rng]
    av = [_dot(jnp.concatenate([a_ak[i], a_rk[i]], axis=0), bdv[i]) for i in rng]
    wu = [_dot(tinv[i], bd2(at[i], av[i][:CHUNK])) for i in rng]
    wu_bf = [wu[i].astype(BF16) for i in rng]
    qo = [_dot(a_rb[i], bd2(wu[i][:, :GROUP], wu[i][:, GROUP:])) for i in rng]
    q = [(rt[i] + qo[i][:, :GROUP]).astype(BF16) for i in rng]
    o0 = [qo[i][:, GROUP:] + av[i][CHUNK:] for i in rng]
    zv = [jnp.concatenate([jnp.zeros_like(v_bf[i]), v_bf[i]], axis=1) for i in rng]
    mg = [_dot_tn(jnp.concatenate([bt[i], kt[i]], axis=0), jnp.concatenate([wu_bf[i], zv[i]], axis=0))
          for i in rng]
    qm = [jnp.concatenate([q[i], jnp.where(bdmask, mg[i][:, :GROUP], 0.0).astype(BF16)], axis=0) for i in rng]
    g_bd = [jnp.where(bdmask, mg[i][:, GROUP:], 0.0) for i in rng]
    return [(qm[i], o0[i], g_bd[i], decay_col[i], bonus[i]) for i in rng]


def _scan_body(reverse, has_prev, n_ctx_blk, blk_per_ctx, blk_per_lat, n_blk, *refs):
    if has_prev:
        (r_ref, k_ref, v_ref, wd_ref, ad_ref, w0_ref, wup_ref, a0_ref, aup_ref, s0_ref, kk_ref, ka_ref, rk_ref,
         op_ref, bp_ref, o_ref, bon_ref, sfin_ref, h_ref) = refs
    else:
        (r_ref, k_ref, v_ref, wd_ref, ad_ref, w0_ref, wup_ref, a0_ref, aup_ref, s0_ref, kk_ref, ka_ref, rk_ref,
         o_ref, bon_ref, sfin_ref, h_ref) = refs
        op_ref = bp_ref = None
    s = pl.program_id(1)
    j = (n_blk - 1 - s) if reverse else s
    is_ctx = j < n_ctx_blk
    first_c = (blk_per_ctx - 1) if reverse else 0
    first_l = (blk_per_lat - 1) if reverse else 0
    start = jnp.where(is_ctx, (j % blk_per_ctx) == first_c, ((j - n_ctx_blk) % blk_per_lat) == first_l)

    @pl.when(start)
    def _():
        h_ref[...] = jnp.where(is_ctx, 0.0, s0_ref[...])

    ones_mask = _head_ones()
    ones_bf = ones_mask.astype(F32).astype(BF16)
    t_i = lax.broadcasted_iota(jnp.int32, (CHUNK, CHUNK), 0)
    s_i = lax.broadcasted_iota(jnp.int32, (CHUNK, CHUNK), 1)
    tri = (s_i >= t_i) if reverse else (s_i <= t_i)
    tri_bf = tri.astype(F32).astype(BF16)
    t4 = lax.broadcasted_iota(jnp.int32, (CHUNK, GROUP), 0)
    s4 = lax.broadcasted_iota(jnp.int32, (CHUNK, GROUP), 1) % HEAD_DIM
    incl4 = (s4 >= t4) if reverse else (s4 <= t4)
    strict4 = (s4 > t4) if reverse else (s4 < t4)
    eye4 = (s4 == t4).astype(F32)
    consts = (ones_bf, ones_mask, tri_bf, strict4, incl4, eye4)

    xw = w0_ref[...] + _dot(jnp.tanh(wd_ref[...]).astype(BF16), wup_ref[...])
    lw_all = -math.exp(-0.5) * jax.nn.sigmoid(xw)
    al_all = jax.nn.sigmoid(a0_ref[...] + _dot(ad_ref[...].astype(BF16), aup_ref[...]))

    n_chunks = SCAN_BLOCK // CHUNK
    order = list(range(n_chunks - 1, -1, -1) if reverse else range(n_chunks))
    sls = [slice(cidx * CHUNK, (cidx + 1) * CHUNK) for cidx in order]
    units = []
    for gi in range(SCAN_GROUPS):
        ln = slice(gi * GROUP, (gi + 1) * GROUP)
        units += [(r_ref[sl, ln], k_ref[sl, ln], v_ref[sl, ln], lw_all[sl, ln], al_all[sl, ln],
                   kk_ref[:, ln], ka_ref[:, ln], rk_ref[:, ln]) for sl in sls]
    flat = _scan_phase1(reverse, units, consts)
    pre = [flat[gi * n_chunks:(gi + 1) * n_chunks] for gi in range(SCAN_GROUPS)]

    h_state = [h_ref[gi] for gi in range(SCAN_GROUPS)]
    for ci, sl in enumerate(sls):
        for gi in range(SCAN_GROUPS):
            ln = slice(gi * GROUP, (gi + 1) * GROUP)
            qm, o0, g_bd, decay_col, bonus = pre[gi][ci]
            y = _dot(qm, h_state[gi].astype(BF16))
            o = y[:CHUNK] + o0
            h_state[gi] = decay_col * h_state[gi] + y[CHUNK:] + g_bd
            if has_prev:
                o = o + op_ref[sl, ln]
                bonus = bonus + bp_ref[sl, ln]
            o_ref[sl, ln] = o
            bon_ref[sl, ln] = bonus
    for gi in range(SCAN_GROUPS):
        h_ref[gi] = h_state[gi]
        sfin_ref[gi] = h_state[gi]


def _stage_scan(reverse, u, w0, w_up_bf, a0, a_up_bf, s0_bd, k_k, k_a, r_k, prev,
                n_ctx_seq, ctx_len, n_lat_seq, lat_len):
    t = u.shape[0]
    blk_per_ctx = ctx_len // SCAN_BLOCK
    blk_per_lat = lat_len // SCAN_BLOCK
    n_ctx_blk = n_ctx_seq * blk_per_ctx
    n_blk = t // SCAN_BLOCK
    has_prev = prev is not None

    def jmap(s):
        return (n_blk - 1 - s) if reverse else s

    width = SCAN_GROUPS * GROUP

    def tok(colbase):
        return pl.BlockSpec((SCAN_BLOCK, width), lambda g, s: (jmap(s), colbase // width + g))

    def s0_map(g, s):
        j = jmap(s)
        b = jnp.clip((j - n_ctx_blk) // blk_per_lat, 0, n_lat_seq - 1)
        return (b, g, 0, 0)

    def sfin_map(g, s):
        j = jmap(s)
        b = jnp.where(j < n_ctx_blk, j // blk_per_ctx, n_ctx_seq)
        return (b, g, 0, 0)

    d = 1 if reverse else 0
    lora = lambda slot: pl.BlockSpec((SCAN_BLOCK, LORA_PAD), lambda g, s: (jmap(s), COL_LORA // LORA_PAD + slot))
    par = pl.BlockSpec((1, width), lambda g, s: (0, g))
    dpar = pl.BlockSpec((None, 1, width), lambda g, s: (d, 0, g))
    dup = pl.BlockSpec((None, LORA_PAD, width), lambda g, s: (d, 0, g))
    state_spec = lambda imap: pl.BlockSpec((None, SCAN_GROUPS, GROUP, GROUP), imap)
    in_specs = [tok(COL_R), tok(COL_K), tok(COL_V), lora(d), lora(2 + d), dpar, dup, dpar, dup,
                state_spec(s0_map), par, par, par]
    args = [u, u, u, u, u, w0[:, None, :], w_up_bf, a0[:, None, :], a_up_bf, s0_bd, k_k, k_a, r_k]
    if has_prev:
        in_specs += [tok(0), tok(0)]
        args += list(prev)
    out = jax.ShapeDtypeStruct((t, D_B), F32)
    body = functools.partial(_scan_body, reverse, has_prev, n_ctx_blk, blk_per_ctx, blk_per_lat, n_blk)
    return pl.pallas_call(
        body,
        out_shape=(out, out, jax.ShapeDtypeStruct((n_ctx_seq + 1, N_GROUPS, GROUP, GROUP), F32)),
        grid=(N_GROUPS // SCAN_GROUPS, n_blk),
        in_specs=in_specs,
        out_specs=(tok(0), tok(0), state_spec(sfin_map)),
        scratch_shapes=[pltpu.VMEM((SCAN_GROUPS, GROUP, GROUP), F32)],
        compiler_params=_cparams(("parallel", "arbitrary")),
        name="rwkv_scan_bwd" if reverse else "rwkv_scan_fwd",
    )(*args)


def _post_body(o_ref, bon_ref, lora_ref, lng_ref, lnb_ref, gup_ref, wo_ref, y_ref):
    ones_bf = _head_ones().astype(F32).astype(BF16)
    o = o_ref[...]
    mu = _headsum(o, ones_bf) * (1.0 / HEAD_DIM)
    d = o - mu
    var = _headsum(d * d, ones_bf) * (1.0 / HEAD_DIM)
    on = d * lax.rsqrt(var + GN_EPS) * lng_ref[...] + lnb_ref[...] + bon_ref[...]
    gd = lora_ref[:, 4 * LORA_PAD:4 * LORA_PAD + LORA_G]
    g = _dot(jax.nn.sigmoid(gd).astype(BF16), gup_ref[...])
    y_ref[...] = _dot((on * g).astype(BF16), wo_ref[...])


def _stage_post(o, bon, u, ln_g, ln_b, g_up_bf, w_o_bf):
    t = o.shape[0]
    tm = 256
    return pl.pallas_call(
        _post_body,
        out_shape=jax.ShapeDtypeStruct((t, D_MODEL), F32),
        grid=(t // tm,),
        in_specs=[pl.BlockSpec((tm, D_B), lambda i: (i, 0)),
                  pl.BlockSpec((tm, D_B), lambda i: (i, 0)),
                  pl.BlockSpec((tm, LORA_BLOCK), lambda i: (i, COL_LORA // LORA_BLOCK)),
                  pl.BlockSpec((1, D_B), lambda i: (0, 0)),
                  pl.BlockSpec((1, D_B), lambda i: (0, 0)),
                  pl.BlockSpec((LORA_G, D_B), lambda i: (0, 0)),
                  pl.BlockSpec((D_B, D_MODEL), lambda i: (0, 0))],
        out_specs=pl.BlockSpec((tm, D_MODEL), lambda i: (i, 0)),
        compiler_params=_cparams(("parallel",)),
        name="rwkv_post",
    )(o, bon, u, ln_g, ln_b, g_up_bf, w_o_bf)


def _merge_body(ya_ref, yb_ref, ga_ref, gb_ref, x_ref, gt_ref, sh_ref, sc_ref, g_ref, w_ref, x1_ref, h2_ref):
    m = jax.nn.sigmoid(ga_ref[...]) * ya_ref[...] + jax.nn.sigmoid(gb_ref[...]) * yb_ref[...]
    x1 = x_ref[...] + gt_ref[...] * _dot(m.astype(BF16), w_ref[...])
    x1_ref[...] = x1
    h2_ref[...] = _rms(x1) * g_ref[...] * (1.0 + sc_ref[...]) + sh_ref[...]


def _stage_merge(y_a, y_b, u, x_all, mod4, norm2_g, w_mix_bf, n_ctx_tok, lat_len):
    t = x_all.shape[0]
    tm = 256
    full = lambda col: pl.BlockSpec((tm, D_MODEL), lambda i: (i, col // D_MODEL))
    out = jax.ShapeDtypeStruct((t, D_MODEL), F32)
    return pl.pallas_call(
        _merge_body,
        out_shape=(out, out),
        grid=(t // tm,),
        in_specs=[full(0), full(0), full(COL_GA), full(COL_GB), full(0),
                  _mod_spec(2, tm, n_ctx_tok, lat_len),
                  _mod_spec(3, tm, n_ctx_tok, lat_len),
                  _mod_spec(4, tm, n_ctx_tok, lat_len),
                  pl.BlockSpec((1, D_MODEL), lambda i: (0, 0)),
                  pl.BlockSpec((D_MODEL, D_MODEL), lambda i: (0, 0))],
        out_specs=(full(0), full(0)),
        compiler_params=_cparams(("parallel",)),
        name="merge_norm2",
    )(y_a, y_b, u, u, x_all, mod4, mod4, mod4, norm2_g, w_mix_bf)


TOPK_TOKENS = 256


def _extract_topk(s, pos, limit, payload=None):
    vals, outs = [], []
    for _ in range(PEER_TOPK):
        m = jnp.max(s, axis=0, keepdims=True)
        pm = jnp.min(jnp.where(s == m, pos, limit), axis=0, keepdims=True)
        sel = pos == pm
        vals.append(m)
        if payload is None:
            outs.append(pm)
        else:
            outs.append(jnp.max(jnp.where(sel, payload, -1), axis=0, keepdims=True))
        s = jnp.where(sel, -jnp.inf, s)
    return jnp.concatenate(vals, axis=0), jnp.concatenate(outs, axis=0)


def _sort_network(n):
    def merge(lo, hi, r):
        step = r * 2
        if step < hi - lo:
            yield from merge(lo, hi, step)
            yield from merge(lo + r, hi, step)
            yield from [(i, i + r) for i in range(lo + r, hi - r, step)]
        else:
            yield (lo, lo + r)

    def sort(lo, hi):
        if hi - lo >= 1:
            mid = lo + (hi - lo) // 2
            yield from sort(lo, mid)
            yield from sort(mid + 1, hi)
            yield from merge(lo, hi, 1)

    return list(sort(0, n - 1))


SUBLANES = 8


def _topk_keys(s):
    n = N_KEYS // SUBLANES
    assert n == PEER_TOPK
    val = [s[SUBLANES * g:SUBLANES * (g + 1)] for g in range(n)]
    row = lax.broadcasted_iota(jnp.int32, (SUBLANES, s.shape[1]), 0)
    idx = [row + SUBLANES * g for g in range(n)]
    for a, b in _sort_network(n):
        first = (val[a] > val[b]) | ((val[a] == val[b]) & (idx[a] < idx[b]))
        val[a], val[b] = jnp.where(first, val[a], val[b]), jnp.where(first, val[b], val[a])
        idx[a], idx[b] = jnp.where(first, idx[a], idx[b]), jnp.where(first, idx[b], idx[a])
    vals, outs = [], []
    for rnd in range(PEER_TOPK):
        m = jnp.max(val[0], axis=0, keepdims=True)
        pm = jnp.min(jnp.where(val[0] == m, idx[0], N_KEYS), axis=0, keepdims=True)
        vals.append(m)
        outs.append(pm)
        win = idx[0] == pm
        for g in range(PEER_TOPK - 1 - rnd):
            val[g] = jnp.where(win, val[g + 1], val[g])
            idx[g] = jnp.where(win, idx[g + 1], idx[g])
    return jnp.concatenate(vals, axis=0), jnp.concatenate(outs, axis=0)


def _topk_body(h_ref, wq_ref, keys_ref, eid_ref, gate_ref, q_ref, eid_s, gate_s):
    q_ref[...] = _dot_nt(wq_ref[...], h_ref[...].astype(BF16))
    half = PEER_TOPK // 2
    iota = lambda rows: lax.broadcasted_iota(jnp.int32, (rows, TOPK_TOKENS), 0)
    cpos = jnp.concatenate([iota(PEER_TOPK)] + [iota(half) + PEER_TOPK * i for i in range(1, half)]
                           + [(iota(half) + half) * PEER_TOPK], axis=0)

    def head(hd, carry):
        tops = []
        for p in range(2):
            off = pl.multiple_of(hd * (2 * HALF_KEY) + p * HALF_KEY, HALF_KEY)
            qs = q_ref[pl.ds(off, HALF_KEY), :].astype(BF16)
            s = _dot(keys_ref[p * PEER_HEADS + hd], qs)
            tops.append(_topk_keys(s))
        (s1, i1), (s2, i2) = tops
        cand = jnp.concatenate([s1[0:1] + s2] + [s1[i:i + 1] + s2[:half] for i in range(1, half)]
                               + [s1[half:] + s2[0:1]], axis=0)
        ids = jnp.concatenate([i1[0:1] * N_KEYS + i2] + [i1[i:i + 1] * N_KEYS + i2[:half] for i in range(1, half)]
                              + [i1[half:] * N_KEYS + i2[0:1]], axis=0)
        tv, te = _extract_topk(cand, cpos, PEER_TOPK * PEER_TOPK, payload=ids)
        ex = jnp.exp(tv - tv[0:1])
        gate = ex / jnp.sum(ex, axis=0, keepdims=True)
        row = pl.multiple_of(hd * PEER_TOPK, PEER_TOPK)
        eid_s[pl.ds(row, PEER_TOPK), :] = te
        gate_s[pl.ds(row, PEER_TOPK), :] = gate
        return carry

    lax.fori_loop(0, PEER_HEADS, head, 0)
    eid_ref[...] = eid_s[...].T
    gate_ref[...] = gate_s[...].T


def _stage_topk(h2, w_q_t_bf, keys_bf):
    t = h2.shape[0]
    tb = TOPK_TOKENS
    return pl.pallas_call(
        _topk_body,
        out_shape=(jax.ShapeDtypeStruct((t, EXPERTS_PER_TOKEN), jnp.int32),
                   jax.ShapeDtypeStruct((t, EXPERTS_PER_TOKEN), F32)),
        grid=(t // tb,),
        in_specs=[pl.BlockSpec((tb, D_MODEL), lambda i: (i, 0)),
                  pl.BlockSpec((PEER_HEADS * 2 * HALF_KEY, D_MODEL), lambda i: (0, 0)),
                  pl.BlockSpec((2 * PEER_HEADS, N_KEYS, HALF_KEY), lambda i: (0, 0, 0))],
        out_specs=(pl.BlockSpec((tb, EXPERTS_PER_TOKEN), lambda i: (i, 0)),
                   pl.BlockSpec((tb, EXPERTS_PER_TOKEN), lambda i: (i, 0))),
        scratch_shapes=[pltpu.VMEM((PEER_HEADS * 2 * HALF_KEY, tb), F32),
                        pltpu.VMEM((EXPERTS_PER_TOKEN, tb), jnp.int32),
                        pltpu.VMEM((EXPERTS_PER_TOKEN, tb), F32)],
        compiler_params=_cparams(("parallel",)),
        name="peer_topk",
    )(h2, w_q_t_bf, keys_bf)


def _gather_copy(tab_ref, buf, sem, row, expert):
    return pltpu.make_async_copy(tab_ref.at[expert], buf.at[pl.ds(row, 1), :], sem)


def _gather_issue(tab_ref, id_ref, row0, buf, sem):
    for t in range(GATHER_TOKENS):
        for e in range(EXPERTS_PER_TOKEN):
            _gather_copy(tab_ref, buf, sem, t * EXPERTS_PER_TOKEN + e, id_ref[row0 + t, e]).start(priority=e % 2)


def _gather_wait(tab_ref, buf, sem):
    for r in range(GATHER_TOKENS * EXPERTS_PER_TOKEN):
        _gather_copy(tab_ref, buf, sem, r, 0).wait()


def _mix_tokens(buf, gate, h, eye):
    ne = EXPERTS_PER_TOKEN
    n_tiles = D_MODEL // LANE
    outs = []
    for t in range(GATHER_TOKENS):
        rows = slice(t * ne, (t + 1) * ne)
        acc = jnp.zeros((ne, LANE), F32)
        for j in range(n_tiles):
            w = buf[rows, j * LANE:(j + 1) * LANE]
            u = lax.bitcast_convert_type(w << 16, F32)
            acc = acc + u * h[t:t + 1, j * LANE:(j + 1) * LANE]
        pre = jnp.sum(acc, axis=-1, keepdims=True)
        gcol = jnp.sum(jnp.where(eye, gate[t:t + 1, :], 0.0), axis=-1, keepdims=True)
        coef = gcol * jax.nn.gelu(pre, approximate=True)
        cb = jnp.broadcast_to(coef, (ne, LANE))
        tiles = []
        for j in range(n_tiles):
            w = buf[rows, j * LANE:(j + 1) * LANE]
            v = lax.bitcast_convert_type(w & jnp.uint32(0xFFFF0000), F32)
            tiles.append(jnp.sum(cb * v, axis=0, keepdims=True))
        outs.append(jnp.concatenate(tiles, axis=1))
    return jnp.concatenate(outs, axis=0)


def _gather_body(ids_ref, idn_ref, gate_ref, h_ref, x_ref, gt_ref, fg_ref, tab_ref, o_ref, *scratch):
    bufs, sem = scratch[:GATHER_SLOTS], scratch[GATHER_SLOTS]
    i = pl.program_id(0)
    n = pl.num_programs(0)
    tg = GATHER_TOKENS
    eye = (lax.broadcasted_iota(jnp.int32, (EXPERTS_PER_TOKEN, EXPERTS_PER_TOKEN), 0)
           == lax.broadcasted_iota(jnp.int32, (EXPERTS_PER_TOKEN, EXPERTS_PER_TOKEN), 1))

    @pl.when(i == 0)
    def _():
        for k in range(GATHER_AHEAD):
            _gather_issue(tab_ref, ids_ref, k * tg, bufs[k], sem.at[k])

    for k in range(GATHER_SLOTS):
        lo = k * tg
        _gather_wait(tab_ref, bufs[k], sem.at[k])
        nxt = k + GATHER_AHEAD
        if nxt < GATHER_SLOTS:
            _gather_issue(tab_ref, ids_ref, nxt * tg, bufs[nxt], sem.at[nxt])
        else:
            _gather_issue(tab_ref, idn_ref, (nxt - GATHER_SLOTS) * tg, bufs[nxt - GATHER_SLOTS],
                          sem.at[nxt - GATHER_SLOTS])
        mix = _mix_tokens(bufs[k], gate_ref[lo:lo + tg, :], h_ref[lo:lo + tg, :], eye)
        x2 = x_ref[lo:lo + tg, :] + gt_ref[...] * mix
        o_ref[lo:lo + tg, :] = _rms(x2) * fg_ref[...]

    @pl.when(i == n - 1)
    def _():
        for k in range(GATHER_AHEAD):
            _gather_wait(tab_ref, bufs[k], sem.at[k])


def _stage_gather(eid, gate, h2, x1, mod4, final_g, tab, n_ctx_tok, lat_len):
    t = h2.shape[0]
    tg = GATHER_TOKENS
    ts = GATHER_SLOTS * tg
    n = t // ts
    tokspec = lambda w: pl.BlockSpec((ts, w), lambda i: (i, 0))
    rows = tg * EXPERTS_PER_TOKEN
    return pl.pallas_call(
        _gather_body,
        out_shape=jax.ShapeDtypeStruct((t, D_MODEL), F32),
        grid=(n,),
        in_specs=[pl.BlockSpec((ts, EXPERTS_PER_TOKEN), lambda i: (i, 0), memory_space=pltpu.SMEM),
                  pl.BlockSpec((ts, EXPERTS_PER_TOKEN), lambda i: (jnp.minimum(i + 1, n - 1), 0),
                               memory_space=pltpu.SMEM),
                  tokspec(EXPERTS_PER_TOKEN), tokspec(D_MODEL), tokspec(D_MODEL),
                  _mod_spec(5, ts, n_ctx_tok, lat_len),
                  pl.BlockSpec((1, D_MODEL), lambda i: (0, 0)),
                  pl.BlockSpec(memory_space=pl.ANY)],
        out_specs=tokspec(D_MODEL),
        scratch_shapes=[pltpu.VMEM((rows, D_MODEL), jnp.uint32) for _ in range(GATHER_SLOTS)]
                       + [pltpu.SemaphoreType.DMA((GATHER_SLOTS,))],
        compiler_params=_cparams(("arbitrary",)),
        name="peer_gather",
    )(eid, eid, gate, h2, x1, mod4, final_g, tab)


def _pack_body(u_ref, v_ref, o_ref):
    bits = lambda x: lax.bitcast_convert_type(x.astype(BF16).astype(F32), jnp.uint32)
    o_ref[...] = (bits(u_ref[...]) >> 16) | (bits(v_ref[...]) & jnp.uint32(0xFFFF0000))


def _pack_expert_tables(u_tab, v_tab):
    n, d = u_tab.shape
    tn = 512
    spec = pl.BlockSpec((tn, d), lambda i: (i, 0))
    packed = pl.pallas_call(
        _pack_body,
        out_shape=jax.ShapeDtypeStruct((n, d), jnp.uint32),
        grid=(n // tn,),
        in_specs=[spec, spec],
        out_specs=spec,
        compiler_params=_cparams(("parallel",)),
        name="pack_expert_tables",
    )(u_tab, v_tab)
    return packed[:, None, :]


def _pad_rows(w, rows):
    return jnp.pad(w, ((0, 0), (0, rows - w.shape[1]), (0, 0)))


def _layer_weights(w_in, w_up, a_up):
    widths = (D_A, D_A, D_A, D_B, D_B, D_B, LORA_W, LORA_W, LORA_A, LORA_A, LORA_G, D_MODEL, D_MODEL)
    cuts = [0]
    for w in widths:
        cuts.append(cuts[-1] + w)
    seg = [w_in[:, cuts[i]:cuts[i + 1]] for i in range(len(widths))]
    (u_b, u_c, u_x, r, k, v, wd_f, wd_b, ad_f, ad_b, gd, gate_a, gate_b) = seg
    padc = lambda m: jnp.pad(m, ((0, 0), (0, LORA_PAD - m.shape[1])))
    tail = jnp.zeros((w_in.shape[0], LORA_BLOCK - 4 * LORA_PAD - LORA_G), w_in.dtype)
    w_cat = jnp.concatenate([r, k, v, gate_a, gate_b, u_b, u_c, u_x,
                             padc(wd_f), padc(wd_b), padc(ad_f), padc(ad_b), gd, tail], axis=1)
    return w_cat.astype(BF16), _pad_rows(w_up, LORA_PAD).astype(BF16), _pad_rows(a_up, LORA_PAD).astype(BF16)


def _state_to_blockdiag(s0):
    b = s0.shape[0]
    st = jnp.swapaxes(s0, -1, -2).reshape(b, N_GROUPS, HEADS_PER_GROUP, HEAD_DIM, HEAD_DIM)
    eye = jnp.eye(HEADS_PER_GROUP, dtype=s0.dtype)
    bd = jnp.einsum('bgikv,ij->bgikjv', st, eye)
    return bd.reshape(b, N_GROUPS, GROUP, GROUP)


def _blockdiag_to_state(bd):
    b = bd.shape[0]
    x = bd.reshape(b, N_GROUPS, HEADS_PER_GROUP, HEAD_DIM, HEADS_PER_GROUP, HEAD_DIM)
    d = jnp.diagonal(x, axis1=2, axis2=4)
    d = jnp.moveaxis(d, -1, 2)
    return jnp.swapaxes(d, -1, -2).reshape(b, H_B, HEAD_DIM, HEAD_DIM)


def kernel(x_prompt, x_sample, state_rwkv, c, c_ctx, w_ada, b_ada, norm1_g, norm2_g, w_in, conv_w, w_out_a, w0, w_up, a0, a_up, g_up, k_k, k_a, r_k, ln_g, ln_b, w_o, w_mix, w_q, sub_keys, u_tab, v_tab, final_g):
    n_ctx_seq, ctx_len, _ = x_prompt.shape
    n_lat_seq, lat_len, _ = x_sample.shape
    depth = w_in.shape[0]
    n_ctx_tok = n_ctx_seq * ctx_len
    assert depth == 1, "the final norm is fused into the last stage of a single layer"
    assert ctx_len % SCAN_BLOCK == 0 and lat_len % 1024 == 0 and n_ctx_tok % 1024 == 0
    assert lat_len % (GRID_W * (CONV_TILE // GRID_W)) == 0

    x_all = jnp.concatenate([x_prompt.reshape(n_ctx_tok, D_MODEL),
                             x_sample.reshape(n_lat_seq * lat_len, D_MODEL)], axis=0)
    mod_rows = 16
    cc = jnp.zeros((mod_rows, D_MODEL), F32).at[0].set(c_ctx).at[1:1 + n_lat_seq].set(c)
    new_states = []
    for l in range(depth):
        w_cat, w_up_bf, a_up_bf = _layer_weights(w_in[l], w_up[l], a_up[l])
        mod = _stage_mod(cc, w_ada[l].astype(BF16), b_ada[l][None, :])
        mod4 = mod.reshape(mod_rows, 6, 1, D_MODEL)
        u = _stage_inproj(x_all, mod4, norm1_g[l][None, :], w_cat, n_ctx_tok, lat_len)
        y_a = _stage_conv(u, conv_w[l], w_out_a[l].astype(BF16), n_ctx_tok, lat_len)
        kk2, ka2, rk2 = k_k[l][None, :], k_a[l][None, :], r_k[l].reshape(1, D_B)
        seq = (n_ctx_seq, ctx_len, n_lat_seq, lat_len)
        lowrank = (w0[l], w_up_bf, a0[l], a_up_bf)
        o_f, bon_f, sfin_f = _stage_scan(False, u, *lowrank, _state_to_blockdiag(state_rwkv[:, l, 0]),
                                         kk2, ka2, rk2, None, *seq)
        o, bon, sfin_b = _stage_scan(True, u, *lowrank, _state_to_blockdiag(state_rwkv[:, l, 1]),
                                     kk2, ka2, rk2, (o_f, bon_f), *seq)
        new_states.append(jnp.stack([_blockdiag_to_state(sfin_f[:n_ctx_seq]),
                                     _blockdiag_to_state(sfin_b[:n_ctx_seq])], axis=1))
        y_b = _stage_post(o, bon, u, ln_g[l][None, :], ln_b[l][None, :], g_up[l].astype(BF16), w_o[l].astype(BF16))
        x1, h2 = _stage_merge(y_a, y_b, u, x_all, mod4, norm2_g[l][None, :], w_mix[l].astype(BF16),
                              n_ctx_tok, lat_len)
        keys = sub_keys[l].reshape(2 * PEER_HEADS, N_KEYS, HALF_KEY).astype(BF16)
        eid, gate = _stage_topk(h2, w_q[l].T.astype(BF16), keys)
        tab = _pack_expert_tables(u_tab[l], v_tab[l])
        x_all = _stage_gather(eid, gate, h2, x1, mod4, final_g[None, :], tab, n_ctx_tok, lat_len)
    y_prompt = x_all[:n_ctx_tok].reshape(x_prompt.shape)
    y_sample = x_all[n_ctx_tok:].reshape(x_sample.shape)
    return (y_prompt, y_sample, jnp.stack(new_states, axis=1))
```

```python
import functools
import math

import jax
import jax.numpy as jnp
from jax import lax
from jax.experimental import pallas as pl
from jax.experimental.pallas import tpu as pltpu

F32 = jnp.float32
BF16 = jnp.bfloat16

D_MODEL = 2048
GRID_W = 64
D_A = 1024
H_B = 32
HEAD_DIM = 64
D_B = H_B * HEAD_DIM
LORA_W = 96
LORA_A = 96
LORA_G = 256
PEER_HEADS = 8
N_KEYS = 128
HALF_KEY = 128
PEER_TOPK = 16
RMS_EPS = 1e-6
GN_EPS = 64e-5

LANE = 128
CHUNK = 64
SCAN_BLOCK = 256
GROUP = 128
HEADS_PER_GROUP = GROUP // HEAD_DIM
N_GROUPS = D_B // GROUP
SCAN_GROUPS = 8
LORA_PAD = 128
EXPERTS_PER_TOKEN = PEER_HEADS * PEER_TOPK
GATHER_TOKENS = 8
GATHER_SLOTS = 4
GATHER_AHEAD = 2
VMEM_LIMIT = 56 * 1024 * 1024

COL_R, COL_K, COL_V, COL_GA, COL_GB = 0, 2048, 4096, 6144, 8192
COL_UB, COL_UC, COL_UX, COL_LORA = 10240, 11264, 12288, 13312
D_IN_PAD = 14336
LORA_BLOCK = 1024


def _cparams(sem, vmem=VMEM_LIMIT):
    return pltpu.CompilerParams(dimension_semantics=sem, vmem_limit_bytes=vmem)


def _dot(a, b):
    return jnp.dot(a, b, preferred_element_type=F32)


def _dot_nt(a, b):
    return lax.dot_general(a, b, (((1,), (1,)), ((), ())), preferred_element_type=F32)


def _dot_tn(a, b):
    return lax.dot_general(a, b, (((0,), (0,)), ((), ())), preferred_element_type=F32)


def _split(x, parts):
    out = []
    for _ in range(parts):
        p = x.astype(BF16)
        out.append(p)
        x = x - p.astype(F32)
    return out


def _head_ones():
    r = lax.broadcasted_iota(jnp.int32, (GROUP, GROUP), 0) // HEAD_DIM
    c = lax.broadcasted_iota(jnp.int32, (GROUP, GROUP), 1) // HEAD_DIM
    return r == c


def _headsum(x, ones_bf):
    outs = []
    for g in range(x.shape[1] // GROUP):
        xs = x[:, g * GROUP:(g + 1) * GROUP]
        hi, lo = _split(xs, 2)
        outs.append(_dot(hi, ones_bf) + _dot(lo, ones_bf))
    return outs[0] if len(outs) == 1 else jnp.concatenate(outs, axis=1)


def _rms(x):
    return x * lax.rsqrt(jnp.mean(x * x, axis=-1, keepdims=True) + RMS_EPS)


def _mod_body(c_ref, w_ref, b_ref, o_ref):
    c = c_ref[...]
    s = (c * jax.nn.sigmoid(c)).astype(BF16)
    o_ref[...] = _dot(s, w_ref[...]) + b_ref[...]


def _stage_mod(cc, w_ada_bf, b_ada):
    rows, d = cc.shape
    n = w_ada_bf.shape[1]
    tn = 2048
    return pl.pallas_call(
        _mod_body,
        out_shape=jax.ShapeDtypeStruct((rows, n), F32),
        grid=(n // tn,),
        in_specs=[pl.BlockSpec((rows, d), lambda j: (0, 0)),
                  pl.BlockSpec((d, tn), lambda j: (0, j)),
                  pl.BlockSpec((1, tn), lambda j: (0, j))],
        out_specs=pl.BlockSpec((rows, tn), lambda j: (0, j)),
        compiler_params=_cparams(("arbitrary",)),
        name="adaln_mod",
    )(cc, w_ada_bf, b_ada)


def _mod_spec(which, tm, n_ctx_tok, lat_len):
    nct = n_ctx_tok // tm
    per = lat_len // tm

    def imap(i, *_):
        row = jnp.where(i < nct, 0, 1 + (i - nct) // per)
        return (row, which, 0, 0)

    return pl.BlockSpec((None, None, 1, D_MODEL), imap)


def _inproj_body(x_ref, sh_ref, sc_ref, g_ref, w_ref, o_ref, h_ref):
    @pl.when(pl.program_id(1) == 0)
    def _():
        h = _rms(x_ref[...]) * g_ref[...] * (1.0 + sc_ref[...]) + sh_ref[...]
        h_ref[...] = h.astype(BF16)

    o_ref[...] = _dot(h_ref[...], w_ref[...])


def _stage_inproj(x_all, mod4, norm1_g, w_cat, n_ctx_tok, lat_len):
    t = x_all.shape[0]
    tm, tn = 1024, 1024
    return pl.pallas_call(
        _inproj_body,
        out_shape=jax.ShapeDtypeStruct((t, D_IN_PAD), F32),
        grid=(t // tm, D_IN_PAD // tn),
        in_specs=[pl.BlockSpec((tm, D_MODEL), lambda i, j: (i, 0)),
                  _mod_spec(0, tm, n_ctx_tok, lat_len),
                  _mod_spec(1, tm, n_ctx_tok, lat_len),
                  pl.BlockSpec((1, D_MODEL), lambda i, j: (0, 0)),
                  pl.BlockSpec((D_MODEL, tn), lambda i, j: (0, j))],
        out_specs=pl.BlockSpec((tm, tn), lambda i, j: (i, j)),
        scratch_shapes=[pltpu.VMEM((tm, D_MODEL), BF16)],
        compiler_params=_cparams(("parallel", "arbitrary")),
        name="norm1_inproj",
    )(x_all, mod4, mod4, norm1_g, w_cat)


CONV_TILE = 256


def _conv_body(n_ctx_tiles, tiles_per_lat, ub_ref, uc_ref, ux_ref, pc_ref, px_ref, nc_ref, nx_ref,
               cw_ref, wo_ref, y_ref):
    i = pl.program_id(0)
    is_lat = i >= n_ctx_tiles
    jl = (i - n_ctx_tiles) % tiles_per_lat
    half = D_A // 2
    z = uc_ref[...] * ux_ref[...]
    cw = cw_ref[...]
    row = lax.broadcasted_iota(jnp.int32, (CONV_TILE, 1), 0)
    pmask = jnp.where(is_lat, GRID_W - 1, CONV_TILE - 1)
    keep_p = (row & pmask) != 0
    keep_n = (row & pmask) != pmask
    zp = jnp.where(keep_p, pltpu.roll(z, 1, 0), 0.0)
    zn = jnp.where(keep_n, pltpu.roll(z, CONV_TILE - 1, 0), 0.0)
    conv_h = cw[0:1] * zp + cw[1:2] * z + cw[2:3] * zn
    zv = z[:, half:]
    hp = jnp.where(jl != 0, pc_ref[...] * px_ref[...], 0.0)
    hn = jnp.where(jl != tiles_per_lat - 1, nc_ref[...] * nx_ref[...], 0.0)
    vp = jnp.concatenate([hp, zv[:CONV_TILE - GRID_W]], axis=0)
    vn = jnp.concatenate([zv[GRID_W:], hn], axis=0)
    conv_v = cw[0:1, half:] * vp + cw[1:2, half:] * zv + cw[2:3, half:] * vn
    zc = jnp.concatenate([conv_h[:, :half], jnp.where(is_lat, conv_v, conv_h[:, half:])], axis=1)
    y_ref[...] = _dot((ub_ref[...] * zc).astype(BF16), wo_ref[...])


def _stage_conv(u, conv_w, w_out_a_bf, n_ctx_tok, lat_len):
    t = u.shape[0]
    tm = CONV_TILE
    n_ctx_tiles = n_ctx_tok // tm
    tiles_per_lat = lat_len // tm
    rpt = tm // GRID_W
    last_halo = t // GRID_W - 1
    half = D_A // 2
    cb = lambda col: col // D_A
    hb = lambda col: (col + half) // half
    body = functools.partial(_conv_body, n_ctx_tiles, tiles_per_lat)
    prev_map = lambda col: (lambda i: (jnp.maximum(i * rpt - 1, 0), hb(col)))
    next_map = lambda col: (lambda i: (jnp.minimum(i * rpt + rpt, last_halo), hb(col)))
    return pl.pallas_call(
        body,
        out_shape=jax.ShapeDtypeStruct((t, D_MODEL), F32),
        grid=(t // tm,),
        in_specs=[pl.BlockSpec((tm, D_A), lambda i: (i, cb(COL_UB))),
                  pl.BlockSpec((tm, D_A), lambda i: (i, cb(COL_UC))),
                  pl.BlockSpec((tm, D_A), lambda i: (i, cb(COL_UX))),
                  pl.BlockSpec((GRID_W, half), prev_map(COL_UC)),
                  pl.BlockSpec((GRID_W, half), prev_map(COL_UX)),
                  pl.BlockSpec((GRID_W, half), next_map(COL_UC)),
                  pl.BlockSpec((GRID_W, half), next_map(COL_UX)),
                  pl.BlockSpec((3, D_A), lambda i: (0, 0)),
                  pl.BlockSpec((D_A, D_MODEL), lambda i: (0, 0))],
        out_specs=pl.BlockSpec((tm, D_MODEL), lambda i: (i, 0)),
        compiler_params=_cparams(("parallel",)),
        name="short_conv",
    )(u, u, u, u, u, u, u, conv_w, w_out_a_bf)


def _scan_phase1(reverse, units, c):
    ones_bf, bdmask, tri_bf, strict4, incl4, eye4 = c
    n = len(units)
    rng = range(n)
    kk_p = [u[5] for u in units]
    ka_p = [u[6] for u in units]
    rk_p = [u[7] for u in units]

    def bd(x_bf):
        return jnp.concatenate([x_bf] * HEADS_PER_GROUP, axis=0) * ones_bf

    def bd2(x, y):
        return jnp.concatenate([bd(x.astype(BF16)), bd(y.astype(BF16))], axis=1)

    r = [u[0] for u in units]
    k = [u[1] for u in units]
    v = [u[2] for u in units]
    lw = [u[3] for u in units]
    al = [u[4] for u in units]
    kraw = [k[i] * kk_p[i] for i in rng]
    kd = [k[i] * (1.0 + (al[i] - 1.0) * ka_p[i]) for i in rng]
    hs = [_dot(jnp.concatenate(_split(kraw[i] * kraw[i], 2) + _split(r[i] * kd[i] * rk_p[i], 2), axis=0), ones_bf)
          for i in rng]
    kkn = [kraw[i] * lax.rsqrt(hs[i][:CHUNK] + hs[i][CHUNK:2 * CHUNK] + 1e-12) for i in rng]
    b = [kkn[i] * al[i] for i in rng]
    bonus = [(hs[i][2 * CHUNK:3 * CHUNK] + hs[i][3 * CHUNK:]) * v[i] for i in rng]

    parts = [_split(lw[i], 3) for i in rng]
    lc = [sum(_dot(tri_bf, p) for p in parts[i]) for i in rng]
    ltot = [(lc[i][0:1] if reverse else lc[i][CHUNK - 1:CHUNK]) for i in rng]
    decay_col = [jnp.broadcast_to(jnp.exp(ltot[i]), (GROUP, GROUP)).T for i in rng]
    rt = [r[i] * jnp.exp(lc[i]) for i in rng]
    at = [-kkn[i] * jnp.exp(lc[i] - lw[i]) for i in rng]
    einv = [jnp.exp(-lc[i]) for i in rng]
    bh = [b[i] * einv[i] for i in rng]
    kh = [kd[i] * einv[i] for i in rng]
    etail = [jnp.exp(ltot[i] - lc[i]) for i in rng]
    bt = [(b[i] * etail[i]).astype(BF16) for i in rng]
    kt = [(kd[i] * etail[i]).astype(BF16) for i in rng]
    v_bf = [v[i].astype(BF16) for i in rng]

    lhs1 = [jnp.concatenate([at[i], rt[i]], axis=0).astype(BF16) for i in rng]
    ab = [_dot_nt(lhs1[i], bd(bh[i].astype(BF16))) for i in rng]
    ak = [_dot_nt(lhs1[i], bd(kh[i].astype(BF16))) for i in rng]
    a_ab = [jnp.where(strict4, ab[i][:CHUNK], 0.0) for i in rng]
    a_rb = [jnp.where(incl4, ab[i][CHUNK:], 0.0).astype(BF16) for i in rng]
    a_ak = [jnp.where(strict4, ak[i][:CHUNK], 0.0).astype(BF16) for i in rng]
    a_rk = [jnp.where(incl4, ak[i][CHUNK:], 0.0).astype(BF16) for i in rng]

    x = a_ab
    tinv = [eye4 + x[i] for i in rng]
    x = [_dot(x[i].astype(BF16), bd(x[i].astype(BF16))) for i in rng]
    for _ in range(4):
        y = [_dot(jnp.concatenate([x[i], tinv[i]], axis=0).astype(BF16), bd(x[i].astype(BF16))) for i in rng]
        x = [y[i][:CHUNK] for i in rng]
        tinv = [tinv[i] + y[i][CHUNK:] for i in rng]
    tinv = [(tinv[i] + _dot(tinv[i].astype(BF16), bd(x[i].astype(BF16)))).astype(BF16) for i in rng]

    bdv = [bd(v_bf[i]) for i in rng]
    av = [_dot(jnp.concatenate([a_ak[i], a_rk[i]], axis=0), bdv[i]) for i in rng]
    wu = [_dot(tinv[i], bd2(at[i], av[i][:CHUNK])) for i in rng]
    wu_bf = [wu[i].astype(BF16) for i in rng]
    qo = [_dot(a_rb[i], bd2(wu[i][:, :GROUP], wu[i][:, GROUP:])) for i in rng]
    q = [(rt[i] + qo[i][:, :GROUP]).astype(BF16) for i in rng]
    o0 = [qo[i][:, GROUP:] + av[i][CHUNK:] for i in rng]
    zv = [jnp.concatenate([jnp.zeros_like(v_bf[i]), v_bf[i]], axis=1) for i in rng]
    mg = [_dot_tn(jnp.concatenate([bt[i], kt[i]], axis=0), jnp.concatenate([wu_bf[i], zv[i]], axis=0))
          for i in rng]
    qm = [jnp.concatenate([q[i], jnp.where(bdmask, mg[i][:, :GROUP], 0.0).astype(BF16)], axis=0) for i in rng]
    g_bd = [jnp.where(bdmask, mg[i][:, GROUP:], 0.0) for i in rng]
    return [(qm[i], o0[i], g_bd[i], decay_col[i], bonus[i]) for i in rng]


def _scan_body(reverse, has_prev, n_ctx_blk, blk_per_ctx, blk_per_lat, n_blk, *refs):
    if has_prev:
        (r_ref, k_ref, v_ref, wd_ref, ad_ref, w0_ref, wup_ref, a0_ref, aup_ref, s0_ref, kk_ref, ka_ref, rk_ref,
         op_ref, bp_ref, o_ref, bon_ref, sfin_ref, h_ref) = refs
    else:
        (r_ref, k_ref, v_ref, wd_ref, ad_ref, w0_ref, wup_ref, a0_ref, aup_ref, s0_ref, kk_ref, ka_ref, rk_ref,
         o_ref, bon_ref, sfin_ref, h_ref) = refs
        op_ref = bp_ref = None
    s = pl.program_id(1)
    j = (n_blk - 1 - s) if reverse else s
    is_ctx = j < n_ctx_blk
    first_c = (blk_per_ctx - 1) if reverse else 0
    first_l = (blk_per_lat - 1) if reverse else 0
    start = jnp.where(is_ctx, (j % blk_per_ctx) == first_c, ((j - n_ctx_blk) % blk_per_lat) == first_l)

    @pl.when(start)
    def _():
        h_ref[...] = jnp.where(is_ctx, 0.0, s0_ref[...])

    ones_mask = _head_ones()
    ones_bf = ones_mask.astype(F32).astype(BF16)
    t_i = lax.broadcasted_iota(jnp.int32, (CHUNK, CHUNK), 0)
    s_i = lax.broadcasted_iota(jnp.int32, (CHUNK, CHUNK), 1)
    tri = (s_i >= t_i) if reverse else (s_i <= t_i)
    tri_bf = tri.astype(F32).astype(BF16)
    t4 = lax.broadcasted_iota(jnp.int32, (CHUNK, GROUP), 0)
    s4 = lax.broadcasted_iota(jnp.int32, (CHUNK, GROUP), 1) % HEAD_DIM
    incl4 = (s4 >= t4) if reverse else (s4 <= t4)
    strict4 = (s4 > t4) if reverse else (s4 < t4)
    eye4 = (s4 == t4).astype(F32)
    consts = (ones_bf, ones_mask, tri_bf, strict4, incl4, eye4)

    xw = w0_ref[...] + _dot(jnp.tanh(wd_ref[...]).astype(BF16), wup_ref[...])
    lw_all = -math.exp(-0.5) * jax.nn.sigmoid(xw)
    al_all = jax.nn.sigmoid(a0_ref[...] + _dot(ad_ref[...].astype(BF16), aup_ref[...]))

    n_chunks = SCAN_BLOCK // CHUNK
    order = list(range(n_chunks - 1, -1, -1) if reverse else range(n_chunks))
    sls = [slice(cidx * CHUNK, (cidx + 1) * CHUNK) for cidx in order]
    units = []
    for gi in range(SCAN_GROUPS):
        ln = slice(gi * GROUP, (gi + 1) * GROUP)
        units += [(r_ref[sl, ln], k_ref[sl, ln], v_ref[sl, ln], lw_all[sl, ln], al_all[sl, ln],
                   kk_ref[:, ln], ka_ref[:, ln], rk_ref[:, ln]) for sl in sls]
    flat = _scan_phase1(reverse, units, consts)
    pre = [flat[gi * n_chunks:(gi + 1) * n_chunks] for gi in range(SCAN_GROUPS)]

    h_state = [h_ref[gi] for gi in range(SCAN_GROUPS)]
    for ci, sl in enumerate(sls):
        for gi in range(SCAN_GROUPS):
            ln = slice(gi * GROUP, (gi + 1) * GROUP)
            qm, o0, g_bd, decay_col, bonus = pre[gi][ci]
            y = _dot(qm, h_state[gi].astype(BF16))
            o = y[:CHUNK] + o0
            h_state[gi] = decay_col * h_state[gi] + y[CHUNK:] + g_bd
            if has_prev:
                o = o + op_ref[sl, ln]
                bonus = bonus + bp_ref[sl, ln]
            o_ref[sl, ln] = o
            bon_ref[sl, ln] = bonus
    for gi in range(SCAN_GROUPS):
        h_ref[gi] = h_state[gi]
        sfin_ref[gi] = h_state[gi]


def _stage_scan(reverse, u, w0, w_up_bf, a0, a_up_bf, s0_bd, k_k, k_a, r_k, prev,
                n_ctx_seq, ctx_len, n_lat_seq, lat_len):
    t = u.shape[0]
    blk_per_ctx = ctx_len // SCAN_BLOCK
    blk_per_lat = lat_len // SCAN_BLOCK
    n_ctx_blk = n_ctx_seq * blk_per_ctx
    n_blk = t // SCAN_BLOCK
    has_prev = prev is not None

    def jmap(s):
        return (n_blk - 1 - s) if reverse else s

    width = SCAN_GROUPS * GROUP

    def tok(colbase):
        return pl.BlockSpec((SCAN_BLOCK, width), lambda g, s: (jmap(s), colbase // width + g))

    def s0_map(g, s):
        j = jmap(s)
        b = jnp.clip((j - n_ctx_blk) // blk_per_lat, 0, n_lat_seq - 1)
        return (b, g, 0, 0)

    def sfin_map(g, s):
        j = jmap(s)
        b = jnp.where(j < n_ctx_blk, j // blk_per_ctx, n_ctx_seq)
        return (b, g, 0, 0)

    d = 1 if reverse else 0
    lora = lambda slot: pl.BlockSpec((SCAN_BLOCK, LORA_PAD), lambda g, s: (jmap(s), COL_LORA // LORA_PAD + slot))
    par = pl.BlockSpec((1, width), lambda g, s: (0, g))
    dpar = pl.BlockSpec((None, 1, width), lambda g, s: (d, 0, g))
    dup = pl.BlockSpec((None, LORA_PAD, width), lambda g, s: (d, 0, g))
    state_spec = lambda imap: pl.BlockSpec((None, SCAN_GROUPS, GROUP, GROUP), imap)
    in_specs = [tok(COL_R), tok(COL_K), tok(COL_V), lora(d), lora(2 + d), dpar, dup, dpar, dup,
                state_spec(s0_map), par, par, par]
    args = [u, u, u, u, u, w0[:, None, :], w_up_bf, a0[:, None, :], a_up_bf, s0_bd, k_k, k_a, r_k]
    if has_prev:
        in_specs += [tok(0), tok(0)]
        args += list(prev)
    out = jax.ShapeDtypeStruct((t, D_B), F32)
    body = functools.partial(_scan_body, reverse, has_prev, n_ctx_blk, blk_per_ctx, blk_per_lat, n_blk)
    return pl.pallas_call(
        body,
        out_shape=(out, out, jax.ShapeDtypeStruct((n_ctx_seq + 1, N_GROUPS, GROUP, GROUP), F32)),
        grid=(N_GROUPS // SCAN_GROUPS, n_blk),
        in_specs=in_specs,
        out_specs=(tok(0), tok(0), state_spec(sfin_map)),
        scratch_shapes=[pltpu.VMEM((SCAN_GROUPS, GROUP, GROUP), F32)],
        compiler_params=_cparams(("parallel", "arbitrary")),
        name="rwkv_scan_bwd" if reverse else "rwkv_scan_fwd",
    )(*args)


def _post_body(o_ref, bon_ref, lora_ref, lng_ref, lnb_ref, gup_ref, wo_ref, y_ref):
    ones_bf = _head_ones().astype(F32).astype(BF16)
    o = o_ref[...]
    mu = _headsum(o, ones_bf) * (1.0 / HEAD_DIM)
    d = o - mu
    var = _headsum(d * d, ones_bf) * (1.0 / HEAD_DIM)
    on = d * lax.rsqrt(var + GN_EPS) * lng_ref[...] + lnb_ref[...] + bon_ref[...]
    gd = lora_ref[:, 4 * LORA_PAD:4 * LORA_PAD + LORA_G]
    g = _dot(jax.nn.sigmoid(gd).astype(BF16), gup_ref[...])
    y_ref[...] = _dot((on * g).astype(BF16), wo_ref[...])


def _stage_post(o, bon, u, ln_g, ln_b, g_up_bf, w_o_bf):
    t = o.shape[0]
    tm = 256
    return pl.pallas_call(
        _post_body,
        out_shape=jax.ShapeDtypeStruct((t, D_MODEL), F32),
        grid=(t // tm,),
        in_specs=[pl.BlockSpec((tm, D_B), lambda i: (i, 0)),
                  pl.BlockSpec((tm, D_B), lambda i: (i, 0)),
                  pl.BlockSpec((tm, LORA_BLOCK), lambda i: (i, COL_LORA // LORA_BLOCK)),
                  pl.BlockSpec((1, D_B), lambda i: (0, 0)),
                  pl.BlockSpec((1, D_B), lambda i: (0, 0)),
                  pl.BlockSpec((LORA_G, D_B), lambda i: (0, 0)),
                  pl.BlockSpec((D_B, D_MODEL), lambda i: (0, 0))],
        out_specs=pl.BlockSpec((tm, D_MODEL), lambda i: (i, 0)),
        compiler_params=_cparams(("parallel",)),
        name="rwkv_post",
    )(o, bon, u, ln_g, ln_b, g_up_bf, w_o_bf)


def _merge_body(ya_ref, yb_ref, ga_ref, gb_ref, x_ref, gt_ref, sh_ref, sc_ref, g_ref, w_ref, x1_ref, h2_ref):
    m = jax.nn.sigmoid(ga_ref[...]) * ya_ref[...] + jax.nn.sigmoid(gb_ref[...]) * yb_ref[...]
    x1 = x_ref[...] + gt_ref[...] * _dot(m.astype(BF16), w_ref[...])
    x1_ref[...] = x1
    h2_ref[...] = _rms(x1) * g_ref[...] * (1.0 + sc_ref[...]) + sh_ref[...]


def _post_merge_body(o_ref, bon_ref, lora_ref, lng_ref, lnb_ref, gup_ref, wo_ref, ya_ref, ga_ref, gb_ref, x_ref,
                     gt_ref, sh_ref, sc_ref, g_ref, w_ref, x1_ref, h2_ref, yb_ref):
    _post_body(o_ref, bon_ref, lora_ref, lng_ref, lnb_ref, gup_ref, wo_ref, yb_ref)
    _merge_body(ya_ref, yb_ref, ga_ref, gb_ref, x_ref, gt_ref, sh_ref, sc_ref, g_ref, w_ref, x1_ref, h2_ref)


def _stage_post_merge(o, bon, u, ln_g, ln_b, g_up_bf, w_o_bf, y_a, x_all, mod4, norm2_g, w_mix_bf,
                      n_ctx_tok, lat_len):
    t = x_all.shape[0]
    tm = 256
    full = lambda col: pl.BlockSpec((tm, D_MODEL), lambda i: (i, col // D_MODEL))
    const = lambda shape: pl.BlockSpec(shape, lambda i: (0, 0), pipeline_mode=pl.Buffered(1))
    out = jax.ShapeDtypeStruct((t, D_MODEL), F32)
    return pl.pallas_call(
        _post_merge_body,
        out_shape=(out, out),
        grid=(t // tm,),
        in_specs=[full(0), full(0),
                  pl.BlockSpec((tm, LORA_BLOCK), lambda i: (i, COL_LORA // LORA_BLOCK)),
                  const((1, D_B)), const((1, D_B)), const((LORA_G, D_B)), const((D_B, D_MODEL)),
                  full(0), full(COL_GA), full(COL_GB), full(0),
                  _mod_spec(2, tm, n_ctx_tok, lat_len),
                  _mod_spec(3, tm, n_ctx_tok, lat_len),
                  _mod_spec(4, tm, n_ctx_tok, lat_len),
                  const((1, D_MODEL)), const((D_MODEL, D_MODEL))],
        out_specs=(full(0), full(0)),
        scratch_shapes=[pltpu.VMEM((tm, D_MODEL), F32)],
        compiler_params=_cparams(("parallel",)),
        name="post_merge_norm2",
    )(o, bon, u, ln_g, ln_b, g_up_bf, w_o_bf, y_a, u, u, x_all, mod4, mod4, mod4, norm2_g, w_mix_bf)


def _stage_merge(y_a, y_b, u, x_all, mod4, norm2_g, w_mix_bf, n_ctx_tok, lat_len):
    t = x_all.shape[0]
    tm = 256
    full = lambda col: pl.BlockSpec((tm, D_MODEL), lambda i: (i, col // D_MODEL))
    out = jax.ShapeDtypeStruct((t, D_MODEL), F32)
    return pl.pallas_call(
        _merge_body,
        out_shape=(out, out),
        grid=(t // tm,),
        in_specs=[full(0), full(0), full(COL_GA), full(COL_GB), full(0),
                  _mod_spec(2, tm, n_ctx_tok, lat_len),
                  _mod_spec(3, tm, n_ctx_tok, lat_len),
                  _mod_spec(4, tm, n_ctx_tok, lat_len),
                  pl.BlockSpec((1, D_MODEL), lambda i: (0, 0)),
                  pl.BlockSpec((D_MODEL, D_MODEL), lambda i: (0, 0))],
        out_specs=(full(0), full(0)),
        compiler_params=_cparams(("parallel",)),
        name="merge_norm2",
    )(y_a, y_b, u, u, x_all, mod4, mod4, mod4, norm2_g, w_mix_bf)


TOPK_TOKENS = 256


def _extract_topk(s, pos, limit, payload=None):
    vals, outs = [], []
    for _ in range(PEER_TOPK):
        m = jnp.max(s, axis=0, keepdims=True)
        pm = jnp.min(jnp.where(s == m, pos, limit), axis=0, keepdims=True)
        sel = pos == pm
        vals.append(m)
        if payload is None:
            outs.append(pm)
        else:
            outs.append(jnp.max(jnp.where(sel, payload, -1), axis=0, keepdims=True))
        s = jnp.where(sel, -jnp.inf, s)
    return jnp.concatenate(vals, axis=0), jnp.concatenate(outs, axis=0)


def _sort_network(n):
    def merge(lo, hi, r):
        step = r * 2
        if step < hi - lo:
            yield from merge(lo, hi, step)
            yield from merge(lo + r, hi, step)
            yield from [(i, i + r) for i in range(lo + r, hi - r, step)]
        else:
            yield (lo, lo + r)

    def sort(lo, hi):
        if hi - lo >= 1:
            mid = lo + (hi - lo) // 2
            yield from sort(lo, mid)
            yield from sort(mid + 1, hi)
            yield from merge(lo, hi, 1)

    return list(sort(0, n - 1))


SUBLANES = 8


def _topk_keys(s):
    n = N_KEYS // SUBLANES
    assert n == PEER_TOPK
    val = [s[SUBLANES * g:SUBLANES * (g + 1)] for g in range(n)]
    row = lax.broadcasted_iota(jnp.int32, (SUBLANES, s.shape[1]), 0)
    idx = [row + SUBLANES * g for g in range(n)]
    for a, b in _sort_network(n):
        first = (val[a] > val[b]) | ((val[a] == val[b]) & (idx[a] < idx[b]))
        val[a], val[b] = jnp.where(first, val[a], val[b]), jnp.where(first, val[b], val[a])
        idx[a], idx[b] = jnp.where(first, idx[a], idx[b]), jnp.where(first, idx[b], idx[a])
    vals, outs = [], []
    for rnd in range(PEER_TOPK):
        m = jnp.max(val[0], axis=0, keepdims=True)
        pm = jnp.min(jnp.where(val[0] == m, idx[0], N_KEYS), axis=0, keepdims=True)
        vals.append(m)
        outs.append(pm)
        win = idx[0] == pm
        for g in range(PEER_TOPK - 1 - rnd):
            val[g] = jnp.where(win, val[g + 1], val[g])
            idx[g] = jnp.where(win, idx[g + 1], idx[g])
    return jnp.concatenate(vals, axis=0), jnp.concatenate(outs, axis=0)


def _topk_body(h_ref, wq_ref, keys_ref, eid_ref, gate_ref, q_ref, eid_s, gate_s):
    q_ref[...] = _dot_nt(wq_ref[...], h_ref[...].astype(BF16))
    half = PEER_TOPK // 2
    iota = lambda rows: lax.broadcasted_iota(jnp.int32, (rows, TOPK_TOKENS), 0)
    cpos = jnp.concatenate([iota(PEER_TOPK)] + [iota(half) + PEER_TOPK * i for i in range(1, half)]
                           + [(iota(half) + half) * PEER_TOPK], axis=0)

    def head(hd, carry):
        tops = []
        for p in range(2):
            off = pl.multiple_of(hd * (2 * HALF_KEY) + p * HALF_KEY, HALF_KEY)
            qs = q_ref[pl.ds(off, HALF_KEY), :].astype(BF16)
            s = _dot(keys_ref[p * PEER_HEADS + hd], qs)
            tops.append(_topk_keys(s))
        (s1, i1), (s2, i2) = tops
        cand = jnp.concatenate([s1[0:1] + s2] + [s1[i:i + 1] + s2[:half] for i in range(1, half)]
                               + [s1[half:] + s2[0:1]], axis=0)
        ids = jnp.concatenate([i1[0:1] * N_KEYS + i2] + [i1[i:i + 1] * N_KEYS + i2[:half] for i in range(1, half)]
                              + [i1[half:] * N_KEYS + i2[0:1]], axis=0)
        tv, te = _extract_topk(cand, cpos, PEER_TOPK * PEER_TOPK, payload=ids)
        ex = jnp.exp(tv - tv[0:1])
        gate = ex / jnp.sum(ex, axis=0, keepdims=True)
        row = pl.multiple_of(hd * PEER_TOPK, PEER_TOPK)
        eid_s[pl.ds(row, PEER_TOPK), :] = te
        gate_s[pl.ds(row, PEER_TOPK), :] = gate
        return carry

    lax.fori_loop(0, PEER_HEADS, head, 0)
    eid_ref[...] = eid_s[...].T
    gate_ref[...] = gate_s[...].T


def _stage_topk(h2, w_q_t_bf, keys_bf):
    t = h2.shape[0]
    tb = TOPK_TOKENS
    return pl.pallas_call(
        _topk_body,
        out_shape=(jax.ShapeDtypeStruct((t, EXPERTS_PER_TOKEN), jnp.int32),
                   jax.ShapeDtypeStruct((t, EXPERTS_PER_TOKEN), F32)),
        grid=(t // tb,),
        in_specs=[pl.BlockSpec((tb, D_MODEL), lambda i: (i, 0)),
                  pl.BlockSpec((PEER_HEADS * 2 * HALF_KEY, D_MODEL), lambda i: (0, 0)),
                  pl.BlockSpec((2 * PEER_HEADS, N_KEYS, HALF_KEY), lambda i: (0, 0, 0))],
        out_specs=(pl.BlockSpec((tb, EXPERTS_PER_TOKEN), lambda i: (i, 0)),
                   pl.BlockSpec((tb, EXPERTS_PER_TOKEN), lambda i: (i, 0))),
        scratch_shapes=[pltpu.VMEM((PEER_HEADS * 2 * HALF_KEY, tb), F32),
                        pltpu.VMEM((EXPERTS_PER_TOKEN, tb), jnp.int32),
                        pltpu.VMEM((EXPERTS_PER_TOKEN, tb), F32)],
        compiler_params=_cparams(("parallel",)),
        name="peer_topk",
    )(h2, w_q_t_bf, keys_bf)


def _gather_copy(tab_ref, buf, sem, row, expert):
    return pltpu.make_async_copy(tab_ref.at[expert], buf.at[pl.ds(row, 1), :], sem)


def _gather_issue(tab_ref, id_ref, row0, buf, sem):
    for t in range(GATHER_TOKENS):
        for e in range(EXPERTS_PER_TOKEN):
            _gather_copy(tab_ref, buf, sem, t * EXPERTS_PER_TOKEN + e, id_ref[row0 + t, e]).start(priority=e % 2)


def _gather_wait(tab_ref, buf, sem):
    for r in range(GATHER_TOKENS * EXPERTS_PER_TOKEN):
        _gather_copy(tab_ref, buf, sem, r, 0).wait()


def _mix_tokens(buf, gate, h, eye):
    ne = EXPERTS_PER_TOKEN
    n_tiles = D_MODEL // LANE
    outs = []
    for t in range(GATHER_TOKENS):
        rows = slice(t * ne, (t + 1) * ne)
        acc = jnp.zeros((ne, LANE), F32)
        for j in range(n_tiles):
            w = buf[rows, j * LANE:(j + 1) * LANE]
            u = lax.bitcast_convert_type(w << 16, F32)
            acc = acc + u * h[t:t + 1, j * LANE:(j + 1) * LANE]
        pre = jnp.sum(acc, axis=-1, keepdims=True)
        gcol = jnp.sum(jnp.where(eye, gate[t:t + 1, :], 0.0), axis=-1, keepdims=True)
        coef = gcol * jax.nn.gelu(pre, approximate=True)
        cb = jnp.broadcast_to(coef, (ne, LANE))
        tiles = []
        for j in range(n_tiles):
            w = buf[rows, j * LANE:(j + 1) * LANE]
            v = lax.bitcast_convert_type(w & jnp.uint32(0xFFFF0000), F32)
            tiles.append(jnp.sum(cb * v, axis=0, keepdims=True))
        outs.append(jnp.concatenate(tiles, axis=1))
    return jnp.concatenate(outs, axis=0)


def _gather_body(ids_ref, idn_ref, gate_ref, h_ref, x_ref, gt_ref, fg_ref, tab_ref, o_ref, *scratch):
    bufs, sem = scratch[:GATHER_SLOTS], scratch[GATHER_SLOTS]
    i = pl.program_id(0)
    n = pl.num_programs(0)
    tg = GATHER_TOKENS
    eye = (lax.broadcasted_iota(jnp.int32, (EXPERTS_PER_TOKEN, EXPERTS_PER_TOKEN), 0)
           == lax.broadcasted_iota(jnp.int32, (EXPERTS_PER_TOKEN, EXPERTS_PER_TOKEN), 1))

    @pl.when(i == 0)
    def _():
        for k in range(GATHER_AHEAD):
            _gather_issue(tab_ref, ids_ref, k * tg, bufs[k], sem.at[k])

    for k in range(GATHER_SLOTS):
        lo = k * tg
        _gather_wait(tab_ref, bufs[k], sem.at[k])
        nxt = k + GATHER_AHEAD
        if nxt < GATHER_SLOTS:
            _gather_issue(tab_ref, ids_ref, nxt * tg, bufs[nxt], sem.at[nxt])
        else:
            _gather_issue(tab_ref, idn_ref, (nxt - GATHER_SLOTS) * tg, bufs[nxt - GATHER_SLOTS],
                          sem.at[nxt - GATHER_SLOTS])
        mix = _mix_tokens(bufs[k], gate_ref[lo:lo + tg, :], h_ref[lo:lo + tg, :], eye)
        x2 = x_ref[lo:lo + tg, :] + gt_ref[...] * mix
        o_ref[lo:lo + tg, :] = _rms(x2) * fg_ref[...]

    @pl.when(i == n - 1)
    def _():
        for k in range(GATHER_AHEAD):
            _gather_wait(tab_ref, bufs[k], sem.at[k])


def _stage_gather(eid, gate, h2, x1, mod4, final_g, tab, n_ctx_tok, lat_len):
    t = h2.shape[0]
    tg = GATHER_TOKENS
    ts = GATHER_SLOTS * tg
    ta = GATHER_AHEAD * tg
    n = t // ts
    tokspec = lambda w: pl.BlockSpec((ts, w), lambda i: (i, 0))
    rows = tg * EXPERTS_PER_TOKEN
    nxt_blocks = ts // ta
    return pl.pallas_call(
        _gather_body,
        out_shape=jax.ShapeDtypeStruct((t, D_MODEL), F32),
        grid=(n,),
        in_specs=[pl.BlockSpec((ts, EXPERTS_PER_TOKEN), lambda i: (i, 0), memory_space=pltpu.SMEM),
                  pl.BlockSpec((ta, EXPERTS_PER_TOKEN),
                               lambda i: (jnp.minimum((i + 1) * nxt_blocks, n * nxt_blocks - 1), 0),
                               memory_space=pltpu.SMEM),
                  tokspec(EXPERTS_PER_TOKEN), tokspec(D_MODEL), tokspec(D_MODEL),
                  _mod_spec(5, ts, n_ctx_tok, lat_len),
                  pl.BlockSpec((1, D_MODEL), lambda i: (0, 0)),
                  pl.BlockSpec(memory_space=pl.ANY)],
        out_specs=tokspec(D_MODEL),
        scratch_shapes=[pltpu.VMEM((rows, D_MODEL), jnp.uint32) for _ in range(GATHER_SLOTS)]
                       + [pltpu.SemaphoreType.DMA((GATHER_SLOTS,))],
        compiler_params=_cparams(("arbitrary",)),
        name="peer_gather",
    )(eid, eid, gate, h2, x1, mod4, final_g, tab)


def _pack_body(u_ref, v_ref, o_ref):
    bits = lambda x: lax.bitcast_convert_type(x.astype(BF16).astype(F32), jnp.uint32)
    o_ref[...] = (bits(u_ref[...]) >> 16) | (bits(v_ref[...]) & jnp.uint32(0xFFFF0000))


def _pack_expert_tables(u_tab, v_tab):
    n, d = u_tab.shape
    tn = 512
    spec = pl.BlockSpec((tn, d), lambda i: (i, 0))
    packed = pl.pallas_call(
        _pack_body,
        out_shape=jax.ShapeDtypeStruct((n, d), jnp.uint32),
        grid=(n // tn,),
        in_specs=[spec, spec],
        out_specs=spec,
        compiler_params=_cparams(("parallel",)),
        name="pack_expert_tables",
    )(u_tab, v_tab)
    return packed[:, None, :]


def _pad_rows(w, rows):
    return jnp.pad(w, ((0, 0), (0, rows - w.shape[1]), (0, 0)))


def _layer_weights(w_in, w_up, a_up):
    widths = (D_A, D_A, D_A, D_B, D_B, D_B, LORA_W, LORA_W, LORA_A, LORA_A, LORA_G, D_MODEL, D_MODEL)
    cuts = [0]
    for w in widths:
        cuts.append(cuts[-1] + w)
    seg = [w_in[:, cuts[i]:cuts[i + 1]] for i in range(len(widths))]
    (u_b, u_c, u_x, r, k, v, wd_f, wd_b, ad_f, ad_b, gd, gate_a, gate_b) = seg
    padc = lambda m: jnp.pad(m, ((0, 0), (0, LORA_PAD - m.shape[1])))
    tail = jnp.zeros((w_in.shape[0], LORA_BLOCK - 4 * LORA_PAD - LORA_G), w_in.dtype)
    w_cat = jnp.concatenate([r, k, v, gate_a, gate_b, u_b, u_c, u_x,
                             padc(wd_f), padc(wd_b), padc(ad_f), padc(ad_b), gd, tail], axis=1)
    return w_cat.astype(BF16), _pad_rows(w_up, LORA_PAD).astype(BF16), _pad_rows(a_up, LORA_PAD).astype(BF16)


def _state_to_blockdiag(s0):
    b = s0.shape[0]
    st = jnp.swapaxes(s0, -1, -2).reshape(b, N_GROUPS, HEADS_PER_GROUP, HEAD_DIM, HEAD_DIM)
    eye = jnp.eye(HEADS_PER_GROUP, dtype=s0.dtype)
    bd = jnp.einsum('bgikv,ij->bgikjv', st, eye)
    return bd.reshape(b, N_GROUPS, GROUP, GROUP)


def _blockdiag_to_state(bd):
    b = bd.shape[0]
    x = bd.reshape(b, N_GROUPS, HEADS_PER_GROUP, HEAD_DIM, HEADS_PER_GROUP, HEAD_DIM)
    d = jnp.diagonal(x, axis1=2, axis2=4)
    d = jnp.moveaxis(d, -1, 2)
    return jnp.swapaxes(d, -1, -2).reshape(b, H_B, HEAD_DIM, HEAD_DIM)


def kernel(x_prompt, x_sample, state_rwkv, c, c_ctx, w_ada, b_ada, norm1_g, norm2_g, w_in, conv_w, w_out_a, w0, w_up, a0, a_up, g_up, k_k, k_a, r_k, ln_g, ln_b, w_o, w_mix, w_q, sub_keys, u_tab, v_tab, final_g):
    n_ctx_seq, ctx_len, _ = x_prompt.shape
    n_lat_seq, lat_len, _ = x_sample.shape
    depth = w_in.shape[0]
    n_ctx_tok = n_ctx_seq * ctx_len
    assert depth == 1, "the final norm is fused into the last stage of a single layer"
    assert ctx_len % SCAN_BLOCK == 0 and lat_len % 1024 == 0 and n_ctx_tok % 1024 == 0
    assert lat_len % (GRID_W * (CONV_TILE // GRID_W)) == 0

    x_all = jnp.concatenate([x_prompt.reshape(n_ctx_tok, D_MODEL),
                             x_sample.reshape(n_lat_seq * lat_len, D_MODEL)], axis=0)
    mod_rows = 16
    cc = jnp.zeros((mod_rows, D_MODEL), F32).at[0].set(c_ctx).at[1:1 + n_lat_seq].set(c)
    new_states = []
    for l in range(depth):
        w_cat, w_up_bf, a_up_bf = _layer_weights(w_in[l], w_up[l], a_up[l])
        mod = _stage_mod(cc, w_ada[l].astype(BF16), b_ada[l][None, :])
        mod4 = mod.reshape(mod_rows, 6, 1, D_MODEL)
        u = _stage_inproj(x_all, mod4, norm1_g[l][None, :], w_cat, n_ctx_tok, lat_len)
        y_a = _stage_conv(u, conv_w[l], w_out_a[l].astype(BF16), n_ctx_tok, lat_len)
        kk2, ka2, rk2 = k_k[l][None, :], k_a[l][None, :], r_k[l].reshape(1, D_B)
        seq = (n_ctx_seq, ctx_len, n_lat_seq, lat_len)
        lowrank = (w0[l], w_up_bf, a0[l], a_up_bf)
        o_f, bon_f, sfin_f = _stage_scan(False, u, *lowrank, _state_to_blockdiag(state_rwkv[:, l, 0]),
                                         kk2, ka2, rk2, None, *seq)
        o, bon, sfin_b = _stage_scan(True, u, *lowrank, _state_to_blockdiag(state_rwkv[:, l, 1]),
                                     kk2, ka2, rk2, (o_f, bon_f), *seq)
        new_states.append(jnp.stack([_blockdiag_to_state(sfin_f[:n_ctx_seq]),
                                     _blockdiag_to_state(sfin_b[:n_ctx_seq])], axis=1))
        x1, h2 = _stage_post_merge(o, bon, u, ln_g[l][None, :], ln_b[l][None, :], g_up[l].astype(BF16),
                                   w_o[l].astype(BF16), y_a, x_all, mod4, norm2_g[l][None, :],
                                   w_mix[l].astype(BF16), n_ctx_tok, lat_len)
        keys = sub_keys[l].reshape(2 * PEER_HEADS, N_KEYS, HALF_KEY).astype(BF16)
        eid, gate = _stage_topk(h2, w_q[l].T.astype(BF16), keys)
        tab = _pack_expert_tables(u_tab[l], v_tab[l])
        x_all = _stage_gather(eid, gate, h2, x1, mod4, final_g[None, :], tab, n_ctx_tok, lat_len)
    y_prompt = x_all[:n_ctx_tok].reshape(x_prompt.shape)
    y_sample = x_all[n_ctx_tok:].reshape(x_sample.shape)
    return (y_prompt, y_sample, jnp.stack(new_states, axis=1))
```
